```python
import jax, jax.numpy as jnp
from jax import lax
import numpy as np

D_MODEL = 1024
BATCH = 1
SEQ = 16384
DEPTH = 4

D_MIX = D_MODEL
D_REC = D_MIX // 2
D_SGU = D_MIX - D_REC
REC_EXPAND = 128
H_REC = D_REC // REC_EXPAND
DK_REC = REC_EXPAND
DV_REC = D_REC // H_REC
CHUNK_REC = 64
SGU_CHUNK = 128
H_SGU = 4
DH_SGU = D_SGU // H_SGU
D_IN = 4 * D_REC + 2 * D_SGU
N_GROUPS = 4
EXP_PER_GROUP = 8
N_EXPERTS = N_GROUPS * EXP_PER_GROUP
TOP_K = 2
D_EXPERT = 512
MOE_BLOCK = 128
P_DIM = 256
EPS = 1e-6

kernel_name = 'hybrid_hgrn2_chunksgu_hmoe'


def rmsnorm(x, g):
    xf = x.astype(jnp.float32)
    y = xf * lax.rsqrt(jnp.mean(xf * xf, axis=-1, keepdims=True) + EPS)
    return (y * g.astype(jnp.float32)).astype(x.dtype)


def hgrn2(q, fz, i, g, lb, o_gain):
    B, S, _ = q.shape
    dt = q.dtype
    nc = S // CHUNK_REC
    q = jax.nn.silu(q.astype(jnp.float32))
    fz = fz.astype(jnp.float32)
    logf = jnp.logaddexp(jnp.log(lb), jnp.log1p(-lb) + jax.nn.log_sigmoid(fz))
    k = (1.0 - lb) * jax.nn.sigmoid(-fz)

    def to_chunks(t, d):
        return t.reshape(B, nc, CHUNK_REC, H_REC, d).transpose(1, 0, 3, 2, 4)

    xs = (to_chunks(q, DK_REC), to_chunks(k, DK_REC), to_chunks(logf, DK_REC),
          to_chunks(i.astype(jnp.float32), DV_REC))
    causal = jnp.tril(jnp.ones((CHUNK_REC, CHUNK_REC), dtype=bool))[:, :, None]

    def step(state, c):
        qc, kc, lfc, ic = c
        b = jnp.cumsum(lfc, axis=2)
        o_inter = jnp.einsum('bhtk,bhkv->bhtv', qc * jnp.exp(b), state)
        decay = jnp.exp(jnp.where(causal, b[:, :, :, None, :] - b[:, :, None, :, :], -jnp.inf))
        scores = jnp.einsum('bhtk,bhsk,bhtsk->bhts', qc, kc, decay)
        o_intra = jnp.einsum('bhts,bhsv->bhtv', scores, ic)
        b_last = b[:, :, -1:, :]
        state = (jnp.exp(b_last[:, :, 0, :, None]) * state
                 + jnp.einsum('bhsk,bhsv->bhkv', kc * jnp.exp(b_last - b), ic))
        return state, o_inter + o_intra

    s0 = jnp.zeros((B, H_REC, DK_REC, DV_REC), jnp.float32)
    _, o = lax.scan(step, s0, xs)
    o = o.transpose(1, 0, 3, 2, 4).reshape(B, S, H_REC, DV_REC)
    o = o * lax.rsqrt(jnp.mean(o * o, axis=-1, keepdims=True) + EPS) * o_gain.astype(jnp.float32)
    o = o.reshape(B, S, D_REC) * jax.nn.silu(g.astype(jnp.float32))
    return o.astype(dt)


def chunk_sgu(u, v, ln_g, ln_b, w_s, b_s):
    B, S, _ = u.shape
    dt = u.dtype
    u = jax.nn.gelu(u.astype(jnp.float32))
    v = jax.nn.gelu(v.astype(jnp.float32))
    mu = jnp.mean(v, axis=-1, keepdims=True)
    var = jnp.mean(jnp.square(v - mu), axis=-1, keepdims=True)
    v = (v - mu) * lax.rsqrt(var + EPS) * ln_g.astype(jnp.float32) + ln_b.astype(jnp.float32)
    v = v.reshape(B, S // SGU_CHUNK, SGU_CHUNK, H_SGU, DH_SGU)
    w = w_s.astype(jnp.float32) * jnp.tril(jnp.ones((SGU_CHUNK, SGU_CHUNK), jnp.float32))
    s = jnp.einsum('hts,bnshd->bnthd', w, v) + b_s.astype(jnp.float32).T[:, :, None]
    return (u * s.reshape(B, S, D_SGU)).astype(dt)


def hier_moe(x, w_rg, b_rg, w_re, b_re, w_gate, w_up, w_down):
    B, S, D = x.shape
    T = B * S
    n = T * TOP_K
    xt = x.reshape(T, D)
    lg = (xt @ w_rg).astype(jnp.float32)
    g_sel = jnp.argmax(lg + b_rg.astype(jnp.float32), axis=-1)
    g_w = jnp.take_along_axis(jax.nn.softmax(lg, axis=-1), g_sel[:, None], axis=-1)
    le = (xt @ w_re).astype(jnp.float32).reshape(T, N_GROUPS, EXP_PER_GROUP)
    le = jnp.take_along_axis(le, g_sel[:, None, None], axis=1)[:, 0]
    be = b_re.astype(jnp.float32).reshape(N_GROUPS, EXP_PER_GROUP)[g_sel]
    _, e_sel = lax.top_k(le + be, TOP_K)
    e_w = jnp.take_along_axis(jax.nn.softmax(le, axis=-1), e_sel, axis=-1)
    e_w = e_w / jnp.sum(e_w, axis=-1, keepdims=True)
    w_tok = (g_w * e_w).reshape(n)
    expert = (g_sel[:, None] * EXP_PER_GROUP + e_sel).reshape(n).astype(jnp.int32)
    order = jnp.argsort(expert)
    e_sorted = expert[order]
    tok_sorted = (order // TOP_K).astype(jnp.int32)
    counts = jnp.bincount(expert, length=N_EXPERTS)
    padded = (counts + MOE_BLOCK - 1) // MOE_BLOCK * MOE_BLOCK
    start = jnp.cumsum(counts) - counts
    pend = jnp.cumsum(padded)
    pstart = pend - padded
    dest = pstart[e_sorted] + jnp.arange(n) - start[e_sorted]
    n_blocks = -(-n // MOE_BLOCK) + N_EXPERTS
    cap = n_blocks * MOE_BLOCK
    x_buf = jnp.zeros((cap, D), x.dtype).at[dest].set(xt[tok_sorted])
    tok_buf = jnp.full((cap,), T, jnp.int32).at[dest].set(tok_sorted)
    w_buf = jnp.zeros((cap,), jnp.float32).at[dest].set(w_tok[order])
    block_e = jnp.clip(jnp.searchsorted(pend, jnp.arange(n_blocks) * MOE_BLOCK, side='right'),
                       0, N_EXPERTS - 1)

    def expert_block(args):
        xb, e = args
        h = jax.nn.silu(xb @ w_gate[e]) * (xb @ w_up[e])
        return h @ w_down[e]

    y = lax.map(expert_block, (x_buf.reshape(n_blocks, MOE_BLOCK, D), block_e))
    y = y.reshape(cap, D).astype(jnp.float32) * w_buf[:, None]
    out = jax.ops.segment_sum(y, tok_buf, num_segments=T + 1)[:T]
    return out.reshape(B, S, D).astype(x.dtype)


def setup_inputs(seed: int = 0) -> dict:
    key = jax.random.key(seed)
    ks = jax.random.split(key, 26)
    f32 = jnp.float32

    def nrm(k, shape, scale):
        return jax.random.normal(k, shape, f32) * scale

    def gain(k, shape):
        return 1.0 + 0.02 * jax.random.normal(k, shape, f32)

    return {
        'x': nrm(ks[0], (BATCH, SEQ, D_MODEL), 1.0),
        'p': nrm(ks[1], (DEPTH, BATCH, SEQ, P_DIM), 1.0),
        'norm_mix': gain(ks[2], (DEPTH, D_MODEL)),
        'w_in': nrm(ks[3], (DEPTH, D_MODEL, D_IN), D_MODEL ** -0.5),
        'lb_logits': nrm(ks[4], (DEPTH, D_REC), 0.5),
        'rec_out_gain': gain(ks[5], (DEPTH, DV_REC)),
        'sgu_ln_g': gain(ks[6], (DEPTH, D_SGU)),
        'sgu_ln_b': nrm(ks[7], (DEPTH, D_SGU), 0.01),
        'sgu_w': nrm(ks[8], (DEPTH, H_SGU, SGU_CHUNK, SGU_CHUNK), SGU_CHUNK ** -0.5),
        'sgu_b': 1.0 + nrm(ks[9], (DEPTH, H_SGU, SGU_CHUNK), 0.01),
        'w_out': nrm(ks[10], (DEPTH, D_MIX, D_MODEL), D_MIX ** -0.5),
        'norm_ffn': gain(ks[11], (DEPTH, D_MODEL)),
        'w_rg': nrm(ks[12], (DEPTH, D_MODEL, N_GROUPS), D_MODEL ** -0.5),
        'b_rg': nrm(ks[13], (DEPTH, N_GROUPS), 0.01),
        'w_re': nrm(ks[14], (DEPTH, D_MODEL, N_EXPERTS), D_MODEL ** -0.5),
        'b_re': nrm(ks[15], (DEPTH, N_EXPERTS), 0.01),
        'w_gate': nrm(ks[16], (DEPTH, N_EXPERTS, D_MODEL, D_EXPERT), D_MODEL ** -0.5),
        'w_up': nrm(ks[17], (DEPTH, N_EXPERTS, D_MODEL, D_EXPERT), D_MODEL ** -0.5),
        'w_down': nrm(ks[18], (DEPTH, N_EXPERTS, D_EXPERT, D_MODEL), D_EXPERT ** -0.5),
        'norm_pg': gain(ks[19], (DEPTH, D_MODEL)),
        'w_pg': nrm(ks[20], (DEPTH, D_MODEL, D_MODEL), D_MODEL ** -0.5),
        'w_pe': nrm(ks[21], (DEPTH, P_DIM, D_MODEL), P_DIM ** -0.5),
        'norm_pe': gain(ks[22], (DEPTH, D_MODEL)),
        'norm_f': gain(ks[23], (D_MODEL,)),
    }


def reference(x, p, norm_mix, w_in, lb_logits, rec_out_gain, sgu_ln_g, sgu_ln_b, sgu_w, sgu_b,
              w_out, norm_ffn, w_rg, b_rg, w_re, b_re, w_gate, w_up, w_down,
              norm_pg, w_pg, w_pe, norm_pe, norm_f):
    lb_all = jnp.cumsum(jax.nn.softmax(lb_logits.astype(jnp.float32), axis=0), axis=0)
    lb_all = lb_all - lb_all[0]
    splits = [D_REC, 2 * D_REC, 3 * D_REC, 4 * D_REC, 4 * D_REC + D_SGU]
    h = x
    for l in range(DEPTH):
        hn = rmsnorm(h, norm_mix[l])
        q, fz, i, g, u, v = jnp.split(hn @ w_in[l], splits, axis=-1)
        o_rec = hgrn2(q, fz, i, g, lb_all[l], rec_out_gain[l])
        o_sgu = chunk_sgu(u, v, sgu_ln_g[l], sgu_ln_b[l], sgu_w[l], sgu_b[l])
        h = h + jnp.concatenate([o_rec, o_sgu], axis=-1) @ w_out[l]
        h = h + hier_moe(rmsnorm(h, norm_ffn[l]), w_rg[l], b_rg[l], w_re[l], b_re[l],
                         w_gate[l], w_up[l], w_down[l])
        gate = jax.nn.sigmoid((rmsnorm(h, norm_pg[l]) @ w_pg[l]).astype(jnp.float32))
        e = rmsnorm(p[l] @ w_pe[l], norm_pe[l]).astype(jnp.float32)
        h = h + (gate * e).astype(h.dtype)
    return rmsnorm(h, norm_f)
```

```python
import functools

import numpy as np
import jax
import jax.numpy as jnp
from jax import lax
from jax.experimental import pallas as pl
from jax.experimental.pallas import tpu as pltpu

F32 = jnp.float32
BF16 = jnp.bfloat16

EPS = 1e-6
D_REC = 512
D_SGU = 512
H_REC = 4
DK_REC = 128
CHUNK_REC = 64
SGU_CHUNK = 128
H_SGU = 4
N_GROUPS = 4
EXP_PER_GROUP = 8
N_EXPERTS = N_GROUPS * EXP_PER_GROUP
ROUTER_LANES = 128
EXPERT_LANE0 = N_GROUPS

TM_MIX = 256
TM_ROUTE = 512
TM_COMB = 256
BM_EXP = 256
DISPATCH_CHUNK = 512
VMEM_LIMIT = 48 * 1024 * 1024

_LEVELS = (2, 4, 8, 16, 32, 64)


def _decay_selectors():
    t = np.arange(CHUNK_REC)[:, None]
    s = np.arange(CHUNK_REC)[None, :]
    mats = [((t // m == s // m) & (s <= t)) for m in _LEVELS]
    mats += [((t // m == s // m) & (s > t)) for m in _LEVELS]
    return np.concatenate(mats, axis=0).astype(np.float32)


def _pair_levels():
    t = np.arange(CHUNK_REC)[:, None]
    s = np.arange(CHUNK_REC)[None, :]
    x = t ^ s
    lev = np.floor(np.log2(np.maximum(x, 1))).astype(np.int32)
    lev = np.where(t == s, -1, lev)
    lev = np.where(s > t, -2, lev)
    return lev.astype(np.int32)


def _rms(x, g):
    return x * lax.rsqrt(jnp.mean(x * x, axis=-1, keepdims=True) + EPS) * g


def _sigmoid(x):
    return 1.0 / (1.0 + jnp.exp(-x))


def _gelu_tanh(x):
    return x * (0.5 * (1.0 + jnp.tanh(0.7978845608028654 * (x + 0.044715 * (x * x * x)))))


def _dot(a, b):
    return jnp.dot(a, b, preferred_element_type=F32)


def _dot_nt(a, b):
    return lax.dot_general(a, b, (((1,), (1,)), ((), ())), preferred_element_type=F32)


def _dot_tn(a, b):
    return lax.dot_general(a, b, (((0,), (0,)), ((), ())), preferred_element_type=F32)


def _mixer_kernel(h_ref, gmix_ref, win_ref, la_ref, l1_ref, oml_ref, gain_ref, lng_ref, lnb_ref,
                  ws_ref, sb_ref, wout_ref, wsel_ref, lev_ref, o_ref, st_ref, ocat_ref):
    tm = h_ref.shape[0]

    @pl.when(pl.program_id(0) == 0)
    def _():
        st_ref[...] = jnp.zeros_like(st_ref)

    h = h_ref[...]
    hb = _rms(h, gmix_ref[...]).astype(BF16)

    def proj(j):
        return _dot(hb, win_ref[:, j * 512:(j + 1) * 512])

    zq = proj(0)
    q = zq * _sigmoid(zq)
    fz = proj(1)
    iv = proj(2).astype(BF16)
    zg = proj(3)
    gate = zg * _sigmoid(zg)

    ls = jnp.minimum(fz, 0.0) - jnp.log1p(jnp.exp(-jnp.abs(fz)))
    c = l1_ref[...] + ls
    la = la_ref[...]
    lf = jnp.maximum(la, c) + jnp.log1p(jnp.exp(-jnp.abs(la - c)))
    kk = oml_ref[...] * (1.0 / (1.0 + jnp.exp(fz)))

    wsel = wsel_ref[...]
    lev = lev_ref[...]
    gain = gain_ref[...]
    nlev = len(_LEVELS)

    for cidx in range(tm // CHUNK_REC):
        sl = slice(cidx * CHUNK_REC, (cidx + 1) * CHUNK_REC)
        lf_c = lf[sl]
        hi = lf_c.astype(BF16)
        mid = (lf_c - hi.astype(F32)).astype(BF16)
        xs = _dot(wsel, jnp.concatenate([hi, mid], axis=1))
        pw = jnp.exp(xs[:, :D_REC] + xs[:, D_REC:])
        f_c = jnp.exp(lf_c)
        for hh in range(H_REC):
            hs = slice(hh * DK_REC, (hh + 1) * DK_REC)
            q_h = q[sl, hs]
            k_h = kk[sl, hs]
            i_h = iv[sl, hs]
            k_b = k_h.astype(BF16)
            sc = jnp.where(lev == -1, _dot_nt(q_h.astype(BF16), k_b), 0.0)
            sc = sc + jnp.where(lev == 0, _dot_nt((q_h * f_c[:, hs]).astype(BF16), k_b), 0.0)
            for j in range(nlev - 1):
                qd = (q_h * pw[j * 64:(j + 1) * 64, hs]).astype(BF16)
                kd = (k_h * pw[(nlev + j) * 64:(nlev + j + 1) * 64, hs]).astype(BF16)
                sc = sc + jnp.where(lev == j + 1, _dot_nt(qd, kd), 0.0)
            jl = nlev - 1
            qb = (q_h * pw[jl * 64:(jl + 1) * 64, hs]).astype(BF16)
            kd = (k_h * pw[(nlev + jl) * 64:(nlev + jl + 1) * 64, hs]).astype(BF16)
            st = st_ref[hh]
            o = _dot(sc.astype(BF16), i_h) + _dot_nt(qb, st.astype(BF16))
            dec = pw[jl * 64 + 63:jl * 64 + 64, hs]
            st_ref[hh] = st * dec + _dot_tn(i_h, kd)
            o = o * lax.rsqrt(jnp.mean(o * o, axis=-1, keepdims=True) + EPS) * gain[:, hs]
            ocat_ref[sl, hs] = (o * gate[sl, hs]).astype(BF16)

    zu = proj(4)
    zv = proj(5)
    r128 = lax.broadcasted_iota(jnp.int32, (SGU_CHUNK, SGU_CHUNK), 0)
    c128 = lax.broadcasted_iota(jnp.int32, (SGU_CHUNK, SGU_CHUNK), 1)
    wtri = [jnp.where(c128 <= r128, ws_ref[hh], 0.0).astype(BF16) for hh in range(H_SGU)]
    for cidx in range(tm // SGU_CHUNK):
        sl = slice(cidx * SGU_CHUNK, (cidx + 1) * SGU_CHUNK)
        ua = _gelu_tanh(zu[sl])
        va = _gelu_tanh(zv[sl])
        mu = jnp.mean(va, axis=-1, keepdims=True)
        vc = va - mu
        var = jnp.mean(vc * vc, axis=-1, keepdims=True)
        vn = (vc * lax.rsqrt(var + EPS) * lng_ref[...] + lnb_ref[...]).astype(BF16)
        for hh in range(H_SGU):
            hs = slice(hh * 128, (hh + 1) * 128)
            s = _dot(wtri[hh], vn[:, hs]) + sb_ref[:, hs]
            ocat_ref[sl, D_REC + hh * 128:D_REC + (hh + 1) * 128] = (ua[:, hs] * s).astype(BF16)

    o_ref[...] = h + _dot(ocat_ref[...], wout_ref[...])


def _mixer(h, gmix, win, la, l1, oml, gain, lng, lnb, ws, sb, wout, wsel, lev):
    t, d = h.shape
    tm = min(TM_MIX, t)
    const = lambda shape: pl.BlockSpec(shape, lambda i: (0,) * len(shape))
    return pl.pallas_call(
        _mixer_kernel,
        grid=(t // tm,),
        in_specs=[
            pl.BlockSpec((tm, d), lambda i: (i, 0)),
            const(gmix.shape), const(win.shape), const(la.shape), const(l1.shape), const(oml.shape),
            const(gain.shape), const(lng.shape), const(lnb.shape), const(ws.shape), const(sb.shape),
            const(wout.shape), const(wsel.shape), const(lev.shape),
        ],
        out_specs=pl.BlockSpec((tm, d), lambda i: (i, 0)),
        out_shape=jax.ShapeDtypeStruct((t, d), F32),
        scratch_shapes=[pltpu.VMEM((H_REC, DK_REC, DK_REC), F32), pltpu.VMEM((tm, d), BF16)],
        compiler_params=pltpu.CompilerParams(dimension_semantics=("arbitrary",), vmem_limit_bytes=VMEM_LIMIT),
        name="mixer",
    )(h, gmix, win, la, l1, oml, gain, lng, lnb, ws, sb, wout, wsel, lev)


def _router_kernel(h_ref, g_ref, wr_ref, br_ref, xn_ref, meta_ref, cnt_ref, carry_ref):
    tm = h_ref.shape[0]

    @pl.when(pl.program_id(0) == 0)
    def _():
        carry_ref[...] = jnp.zeros_like(carry_ref)

    xn = _rms(h_ref[...], g_ref[...])
    xn_ref[...] = xn
    logits = jnp.dot(xn, wr_ref[...], preferred_element_type=F32, precision=lax.Precision.HIGHEST)
    biased = logits + br_ref[...]
    lane = lax.broadcasted_iota(jnp.int32, (tm, ROUTER_LANES), 1).astype(F32)
    ninf = -jnp.inf
    big = float(ROUTER_LANES)

    def first_argmax(v):
        m = jnp.max(v, axis=-1, keepdims=True)
        return jnp.min(jnp.where(v == m, lane, big), axis=-1, keepdims=True)

    def pick(v, idx):
        return jnp.sum(jnp.where(lane == idx, v, 0.0), axis=-1, keepdims=True)

    is_g = lane < float(N_GROUPS)
    g_sel = first_argmax(jnp.where(is_g, biased, ninf))
    lg = jnp.where(is_g, logits, ninf)
    eg = jnp.exp(lg - jnp.max(lg, axis=-1, keepdims=True))
    g_w = pick(eg, g_sel) / jnp.sum(eg, axis=-1, keepdims=True)

    lo = float(EXPERT_LANE0) + float(EXP_PER_GROUP) * g_sel
    is_e = (lane >= lo) & (lane < lo + float(EXP_PER_GROUP))
    eb = jnp.where(is_e, biased, ninf)
    i1 = first_argmax(eb)
    i2 = first_argmax(jnp.where(lane == i1, ninf, eb))
    le = jnp.where(is_e, logits, ninf)
    ee = jnp.exp(le - jnp.max(le, axis=-1, keepdims=True))
    den = jnp.sum(ee, axis=-1, keepdims=True)
    p1 = pick(ee, i1) / den
    p2 = pick(ee, i2) / den
    w1 = g_w * (p1 / (p1 + p2))
    w2 = g_w * (p2 / (p1 + p2))

    hot1 = lane == i1
    hot2 = lane == i2
    cnt = jnp.where(hot1 | hot2, 1.0, 0.0)
    row = lax.broadcasted_iota(jnp.int32, (tm, tm), 0)
    col = lax.broadcasted_iota(jnp.int32, (tm, tm), 1)
    ltri = jnp.where(col < row, 1.0, 0.0).astype(BF16)
    pos = _dot(ltri, cnt.astype(BF16)) + carry_ref[...]
    r1 = jnp.sum(jnp.where(hot1, pos, 0.0), axis=-1, keepdims=True)
    r2 = jnp.sum(jnp.where(hot2, pos, 0.0), axis=-1, keepdims=True)
    carry = carry_ref[...] + jnp.sum(cnt, axis=0, keepdims=True)
    carry_ref[...] = carry
    cnt_ref[...] = carry

    e1 = i1 - float(EXPERT_LANE0)
    e2 = i2 - float(EXPERT_LANE0)
    meta = jnp.where(lane == 0.0, e1, 0.0)
    for j, v in enumerate((e2, w1, w2, r1, r2), start=1):
        meta = jnp.where(lane == float(j), v, meta)
    meta_ref[...] = meta


def _router(h, g, wr, br):
    t, d = h.shape
    tm = min(TM_ROUTE, t)
    return pl.pallas_call(
        _router_kernel,
        grid=(t // tm,),
        in_specs=[
            pl.BlockSpec((tm, d), lambda i: (i, 0)),
            pl.BlockSpec(g.shape, lambda i: (0, 0)),
            pl.BlockSpec(wr.shape, lambda i: (0, 0)),
            pl.BlockSpec(br.shape, lambda i: (0, 0)),
        ],
        out_specs=[
            pl.BlockSpec((tm, d), lambda i: (i, 0)),
            pl.BlockSpec((tm, ROUTER_LANES), lambda i: (i, 0)),
            pl.BlockSpec((1, ROUTER_LANES), lambda i: (0, 0)),
        ],
        out_shape=[
            jax.ShapeDtypeStruct((t, d), F32),
            jax.ShapeDtypeStruct((t, ROUTER_LANES), F32),
            jax.ShapeDtypeStruct((1, ROUTER_LANES), F32),
        ],
        scratch_shapes=[pltpu.VMEM((1, ROUTER_LANES), F32)],
        compiler_params=pltpu.CompilerParams(dimension_semantics=("arbitrary",), vmem_limit_bytes=VMEM_LIMIT),
        name="router",
    )(h, g, wr, br)


def _dispatch_kernel(d1_ref, d2_ref, zs_ref, zn_ref, xn_hbm, xb_hbm, zrow, sem):
    t = xn_hbm.shape[0]
    chunk = min(DISPATCH_CHUNK, t)
    nchunk = t // chunk
    zrow[...] = jnp.zeros_like(zrow)

    def row_copy(src_row, dst_row):
        return pltpu.make_async_copy(xn_hbm.at[pl.ds(src_row, 1)], xb_hbm.at[pl.ds(dst_row, 1)], sem)

    def zero_copy(dst_row):
        return pltpu.make_async_copy(zrow.at[pl.ds(0, 1)], xb_hbm.at[pl.ds(dst_row, 1)], sem)

    def issue(j):
        def body(r, carry):
            tok = j * chunk + r
            row_copy(tok, d1_ref[tok]).start()
            row_copy(tok, d2_ref[tok]).start()
            return carry
        lax.fori_loop(0, chunk, body, 0)

    def drain():
        def body(r, carry):
            row_copy(0, 0).wait()
            row_copy(0, 0).wait()
            return carry
        lax.fori_loop(0, chunk, body, 0)

    issue(0)

    def step(j, carry):
        issue(j)
        drain()
        return carry

    lax.fori_loop(1, nchunk, step, 0)
    drain()

    def zero_expert(e, carry):
        def body(r, c2):
            zero_copy(zs_ref[e] + r).start()
            return c2
        lax.fori_loop(0, zn_ref[e], body, 0)

        def wbody(r, c2):
            zero_copy(0).wait()
            return c2
        lax.fori_loop(0, zn_ref[e], wbody, 0)
        return carry

    lax.fori_loop(0, N_EXPERTS, zero_expert, 0)


def _dispatch(d1, d2, zs, zn, xn, cap):
    t, d = xn.shape
    return pl.pallas_call(
        _dispatch_kernel,
        grid_spec=pltpu.PrefetchScalarGridSpec(
            num_scalar_prefetch=4,
            grid=(1,),
            in_specs=[pl.BlockSpec(memory_space=pl.ANY)],
            out_specs=pl.BlockSpec(memory_space=pl.ANY),
            scratch_shapes=[pltpu.VMEM((8, d), F32), pltpu.SemaphoreType.DMA],
        ),
        out_shape=jax.ShapeDtypeStruct((cap, d), F32),
        name="dispatch",
    )(d1, d2, zs, zn, xn)


def _expert_kernel(be_ref, nu_ref, x_ref, wg_ref, wu_ref, wd_ref, y_ref, wgb, wub, wdb):
    b = pl.program_id(0)
    used = b < nu_ref[0]

    @pl.when(used)
    def _():
        e = be_ref[b]
        changed = (b == 0) | (e != be_ref[jnp.maximum(b - 1, 0)])

        @pl.when(changed)
        def _():
            wgb[...] = wg_ref[...].astype(BF16)
            wub[...] = wu_ref[...].astype(BF16)
            wdb[...] = wd_ref[...].astype(BF16)

        x = x_ref[...].astype(BF16)
        a = _dot(x, wgb[...])
        u = _dot(x, wub[...])
        hmid = (a * _sigmoid(a)) * u
        y_ref[...] = _dot(hmid.astype(BF16), wdb[...])

    @pl.when(jnp.logical_not(used))
    def _():
        y_ref[...] = jnp.zeros_like(y_ref)


def _experts(block_e, n_used, xb, wg, wu, wd):
    cap, d = xb.shape
    de = wg.shape[-1]
    nb = cap // BM_EXP
    xmap = lambda b, be, nu: (jnp.minimum(b, nu[0] - 1), 0)
    wmap = lambda b, be, nu: (be[b], 0, 0)
    return pl.pallas_call(
        _expert_kernel,
        grid_spec=pltpu.PrefetchScalarGridSpec(
            num_scalar_prefetch=2,
            grid=(nb,),
            in_specs=[
                pl.BlockSpec((BM_EXP, d), xmap),
                pl.BlockSpec((None, d, de), wmap),
                pl.BlockSpec((None, d, de), wmap),
                pl.BlockSpec((None, de, d), wmap),
            ],
            out_specs=pl.BlockSpec((BM_EXP, d), lambda b, be, nu: (b, 0)),
            scratch_shapes=[pltpu.VMEM((d, de), BF16), pltpu.VMEM((d, de), BF16), pltpu.VMEM((de, d), BF16)],
        ),
        out_shape=jax.ShapeDtypeStruct((cap, d), F32),
        compiler_params=pltpu.CompilerParams(dimension_semantics=("arbitrary",), vmem_limit_bytes=VMEM_LIMIT),
        name="experts",
    )(block_e, n_used, xb, wg, wu, wd)


def _combine_kernel(d1_ref, d2_ref, h_ref, meta_ref, p_ref, wpe_ref, npe_ref, npg_ref, wpg_ref, nf_ref,
                    y_hbm, o_ref, y1buf, y2buf, sem, *, final):
    tm = h_ref.shape[0]
    base = pl.program_id(0) * tm

    def row_copy(src_row, buf, r):
        return pltpu.make_async_copy(y_hbm.at[pl.ds(src_row, 1)], buf.at[pl.ds(r, 1)], sem)

    def issue(r, carry):
        row_copy(d1_ref[base + r], y1buf, r).start()
        row_copy(d2_ref[base + r], y2buf, r).start()
        return carry

    lax.fori_loop(0, tm, issue, 0)

    emb = _rms(_dot(p_ref[...].astype(BF16), wpe_ref[...]), npe_ref[...])

    def drain(r, carry):
        row_copy(0, y1buf, r).wait()
        row_copy(0, y2buf, r).wait()
        return carry

    lax.fori_loop(0, tm, drain, 0)

    meta = meta_ref[...]
    h2 = h_ref[...] + (y1buf[...] * meta[:, 2:3] + y2buf[...] * meta[:, 3:4])
    gate = _sigmoid(_dot(_rms(h2, npg_ref[...]).astype(BF16), wpg_ref[...]))
    out = h2 + gate * emb
    if final:
        out = _rms(out, nf_ref[...])
    o_ref[...] = out


def _combine(d1, d2, h, meta, p, wpe, npe, npg, wpg, nf, y, final):
    t, d = h.shape
    tm = min(TM_COMB, t)
    const = lambda shape: pl.BlockSpec(shape, lambda i, a, b: (0,) * len(shape))
    return pl.pallas_call(
        functools.partial(_combine_kernel, final=final),
        grid_spec=pltpu.PrefetchScalarGridSpec(
            num_scalar_prefetch=2,
            grid=(t // tm,),
            in_specs=[
                pl.BlockSpec((tm, d), lambda i, a, b: (i, 0)),
                pl.BlockSpec((tm, ROUTER_LANES), lambda i, a, b: (i, 0)),
                pl.BlockSpec((tm, p.shape[1]), lambda i, a, b: (i, 0)),
                const(wpe.shape), const(npe.shape), const(npg.shape), const(wpg.shape), const(nf.shape),
                pl.BlockSpec(memory_space=pl.ANY),
            ],
            out_specs=pl.BlockSpec((tm, d), lambda i, a, b: (i, 0)),
            scratch_shapes=[pltpu.VMEM((tm, d), F32), pltpu.VMEM((tm, d), F32), pltpu.SemaphoreType.DMA],
        ),
        out_shape=jax.ShapeDtypeStruct((t, d), F32),
        compiler_params=pltpu.CompilerParams(dimension_semantics=("arbitrary",), vmem_limit_bytes=VMEM_LIMIT),
        name="combine",
    )(d1, d2, h, meta, p, wpe, npe, npg, wpg, nf, y)


def _routing_tables(meta, cnt):
    t = meta.shape[0]
    counts = cnt[0, EXPERT_LANE0:EXPERT_LANE0 + N_EXPERTS].astype(jnp.int32)
    padded = (counts + BM_EXP - 1) // BM_EXP * BM_EXP
    pend = jnp.cumsum(padded)
    pstart = pend - padded
    e1 = meta[:, 0].astype(jnp.int32)
    e2 = meta[:, 1].astype(jnp.int32)
    d1 = pstart[e1] + meta[:, 4].astype(jnp.int32)
    d2 = pstart[e2] + meta[:, 5].astype(jnp.int32)
    nb = -(-2 * t // BM_EXP) + N_EXPERTS
    block_e = jnp.clip(jnp.searchsorted(pend, jnp.arange(nb, dtype=jnp.int32) * BM_EXP, side="right"),
                       0, N_EXPERTS - 1).astype(jnp.int32)
    n_used = (pend[-1:] // BM_EXP).astype(jnp.int32)
    return d1, d2, pstart + counts, padded - counts, block_e, n_used, nb * BM_EXP


def kernel(x, p, norm_mix, w_in, lb_logits, rec_out_gain, sgu_ln_g, sgu_ln_b, sgu_w, sgu_b, w_out, norm_ffn, w_rg, b_rg, w_re, b_re, w_gate, w_up, w_down, norm_pg, w_pg, w_pe, norm_pe, norm_f):
    bsz, seq, d = x.shape
    depth = w_in.shape[0]
    assert bsz == 1
    h = x.reshape(seq, d)

    lb = jnp.cumsum(jax.nn.softmax(lb_logits.astype(F32), axis=0), axis=0)
    lb = lb - lb[0]
    la, l1, oml = jnp.log(lb), jnp.log1p(-lb), 1.0 - lb
    wsel = jnp.asarray(_decay_selectors(), BF16)
    lev = jnp.asarray(_pair_levels())
    npad = ROUTER_LANES - N_GROUPS - N_EXPERTS
    row = lambda v: v.reshape(1, -1)

    for l in range(depth):
        h = _mixer(
            h, row(norm_mix[l]), w_in[l].astype(BF16), row(la[l]), row(l1[l]), row(oml[l]),
            row(jnp.tile(rec_out_gain[l], H_REC)), row(sgu_ln_g[l]), row(sgu_ln_b[l]), sgu_w[l],
            jnp.repeat(sgu_b[l].T, D_SGU // H_SGU, axis=1), w_out[l].astype(BF16), wsel, lev)
        wr = jnp.concatenate([w_rg[l], w_re[l], jnp.zeros((d, npad), F32)], axis=1)
        br = row(jnp.concatenate([b_rg[l], b_re[l], jnp.zeros((npad,), F32)]))
        xn, meta, cnt = _router(h, row(norm_ffn[l]), wr, br)
        d1, d2, zs, zn, block_e, n_used, cap = _routing_tables(meta, cnt)
        xb = _dispatch(d1, d2, zs, zn, xn, cap)
        y = _experts(block_e, n_used, xb, w_gate[l], w_up[l], w_down[l])
        h = _combine(d1, d2, h, meta, p[l, 0], w_pe[l].astype(BF16), row(norm_pe[l]), row(norm_pg[l]),
                     w_pg[l].astype(BF16), row(norm_f), y, final=(l == depth - 1))
    return h.reshape(bsz, seq, d)
```

```python
import functools

import numpy as np
import jax
import jax.numpy as jnp
from jax import lax
from jax.experimental import pallas as pl
from jax.experimental.pallas import tpu as pltpu

F32 = jnp.float32
BF16 = jnp.bfloat16
I32 = jnp.int32

EPS = 1e-6
D_REC = 512
D_SGU = 512
H_REC = 4
DK_REC = 128
CHUNK_REC = 64
SGU_CHUNK = 128
H_SGU = 4
N_GROUPS = 4
EXP_PER_GROUP = 8
N_EXPERTS = N_GROUPS * EXP_PER_GROUP
ROUTER_LANES = 128
EXPERT_LANE0 = N_GROUPS
CODE_SHIFT = 16
CODE_ROWS = 8

TM_MIX = 256
TM_ROUTE = 512
TM_COMB = 256
BM_EXP = 256
VMEM_LIMIT = 48 * 1024 * 1024

_LEVELS = (2, 4, 8, 16, 32, 64)


def _decay_selectors():
    t = np.arange(CHUNK_REC)[:, None]
    s = np.arange(CHUNK_REC)[None, :]
    mats = [((t // m == s // m) & (s <= t)) for m in _LEVELS]
    mats += [((t // m == s // m) & (s > t)) for m in _LEVELS]
    return np.concatenate(mats, axis=0).astype(np.float32)


def _pair_levels():
    t = np.arange(CHUNK_REC)[:, None]
    s = np.arange(CHUNK_REC)[None, :]
    x = t ^ s
    lev = np.floor(np.log2(np.maximum(x, 1))).astype(np.int32)
    lev = np.where(t == s, -1, lev)
    lev = np.where(s > t, -2, lev)
    return lev.astype(np.int32)


def _rms(x, g):
    return x * lax.rsqrt(jnp.mean(x * x, axis=-1, keepdims=True) + EPS) * g


def _sigmoid(x):
    return 1.0 / (1.0 + jnp.exp(-x))


def _gelu_tanh(x):
    return x * (0.5 * (1.0 + jnp.tanh(0.7978845608028654 * (x + 0.044715 * (x * x * x)))))


def _dot(a, b):
    return jnp.dot(a, b, preferred_element_type=F32)


def _dot_nt(a, b):
    return lax.dot_general(a, b, (((1,), (1,)), ((), ())), preferred_element_type=F32)


def _dot_tn(a, b):
    return lax.dot_general(a, b, (((0,), (0,)), ((), ())), preferred_element_type=F32)


def _layer_spec(arr, layer, nargs):
    zeros = (0,) * (arr.ndim - 1)
    return pl.BlockSpec((None,) + arr.shape[1:], lambda *a: (layer,) + zeros)


def _mixer_kernel(h_ref, gmix_ref, win_ref, la_ref, l1_ref, oml_ref, gain_ref, lng_ref, lnb_ref,
                  ws_ref, sb_ref, wout_ref, wsel_ref, lev_ref, o_ref, st_ref, ocat_ref):
    tm = h_ref.shape[0]

    @pl.when(pl.program_id(0) == 0)
    def _():
        st_ref[...] = jnp.zeros_like(st_ref)

    h = h_ref[...]
    hb = _rms(h, gmix_ref[...]).astype(BF16)

    def proj(j):
        return _dot(hb, win_ref[:, j * 512:(j + 1) * 512])

    zq = proj(0)
    q = zq * _sigmoid(zq)
    fz = proj(1)
    iv = proj(2).astype(BF16)
    zg = proj(3)
    gate = zg * _sigmoid(zg)

    ls = jnp.minimum(fz, 0.0) - jnp.log1p(jnp.exp(-jnp.abs(fz)))
    c = l1_ref[...] + ls
    la = la_ref[...]
    lf = jnp.maximum(la, c) + jnp.log1p(jnp.exp(-jnp.abs(la - c)))
    kk = oml_ref[...] * (1.0 / (1.0 + jnp.exp(fz)))

    wsel = wsel_ref[...]
    lev = lev_ref[...]
    gain = gain_ref[...]
    nlev = len(_LEVELS)

    for cidx in range(tm // CHUNK_REC):
        sl = slice(cidx * CHUNK_REC, (cidx + 1) * CHUNK_REC)
        lf_c = lf[sl]
        hi = lf_c.astype(BF16)
        mid = (lf_c - hi.astype(F32)).astype(BF16)
        xs = _dot(wsel, jnp.concatenate([hi, mid], axis=1))
        pw = jnp.exp(xs[:, :D_REC] + xs[:, D_REC:])
        f_c = jnp.exp(lf_c)
        for hh in range(H_REC):
            hs = slice(hh * DK_REC, (hh + 1) * DK_REC)
            q_h = q[sl, hs]
            k_h = kk[sl, hs]
            i_h = iv[sl, hs]
            k_b = k_h.astype(BF16)
            sc = jnp.where(lev == -1, _dot_nt(q_h.astype(BF16), k_b), 0.0)
            sc = sc + jnp.where(lev == 0, _dot_nt((q_h * f_c[:, hs]).astype(BF16), k_b), 0.0)
            for j in range(nlev - 1):
                qd = (q_h * pw[j * 64:(j + 1) * 64, hs]).astype(BF16)
                kd = (k_h * pw[(nlev + j) * 64:(nlev + j + 1) * 64, hs]).astype(BF16)
                sc = sc + jnp.where(lev == j + 1, _dot_nt(qd, kd), 0.0)
            jl = nlev - 1
            qb = (q_h * pw[jl * 64:(jl + 1) * 64, hs]).astype(BF16)
            kd = (k_h * pw[(nlev + jl) * 64:(nlev + jl + 1) * 64, hs]).astype(BF16)
            st = st_ref[hh]
            o = _dot(sc.astype(BF16), i_h) + _dot_nt(qb, st.astype(BF16))
            dec = pw[jl * 64 + 63:jl * 64 + 64, hs]
            st_ref[hh] = st * dec + _dot_tn(i_h, kd)
            o = o * lax.rsqrt(jnp.mean(o * o, axis=-1, keepdims=True) + EPS) * gain[:, hs]
            ocat_ref[sl, hs] = (o * gate[sl, hs]).astype(BF16)

    zu = proj(4)
    zv = proj(5)
    r128 = lax.broadcasted_iota(I32, (SGU_CHUNK, SGU_CHUNK), 0)
    c128 = lax.broadcasted_iota(I32, (SGU_CHUNK, SGU_CHUNK), 1)
    wtri = [jnp.where(c128 <= r128, ws_ref[hh], 0.0).astype(BF16) for hh in range(H_SGU)]
    for cidx in range(tm // SGU_CHUNK):
        sl = slice(cidx * SGU_CHUNK, (cidx + 1) * SGU_CHUNK)
        ua = _gelu_tanh(zu[sl])
        va = _gelu_tanh(zv[sl])
        mu = jnp.mean(va, axis=-1, keepdims=True)
        vc = va - mu
        var = jnp.mean(vc * vc, axis=-1, keepdims=True)
        vn = (vc * lax.rsqrt(var + EPS) * lng_ref[...] + lnb_ref[...]).astype(BF16)
        for hh in range(H_SGU):
            hs = slice(hh * 128, (hh + 1) * 128)
            s = _dot(wtri[hh], vn[:, hs]) + sb_ref[:, hs]
            ocat_ref[sl, D_REC + hh * 128:D_REC + (hh + 1) * 128] = (ua[:, hs] * s).astype(BF16)

    o_ref[...] = h + _dot(ocat_ref[...], wout_ref[...])


def _mixer(layer, h, gmix, win, la, l1, oml, gain, lng, lnb, ws, sb, wout, wsel, lev):
    t, d = h.shape
    tm = min(TM_MIX, t)
    per_layer = (gmix, win, la, l1, oml, gain, lng, lnb, ws, sb, wout)
    return pl.pallas_call(
        _mixer_kernel,
        grid=(t // tm,),
        in_specs=[pl.BlockSpec((tm, d), lambda i: (i, 0))]
        + [_layer_spec(a, layer, 1) for a in per_layer]
        + [pl.BlockSpec(wsel.shape, lambda i: (0, 0)), pl.BlockSpec(lev.shape, lambda i: (0, 0))],
        out_specs=pl.BlockSpec((tm, d), lambda i: (i, 0)),
        out_shape=jax.ShapeDtypeStruct((t, d), F32),
        scratch_shapes=[pltpu.VMEM((H_REC, DK_REC, DK_REC), F32), pltpu.VMEM((tm, d), BF16)],
        compiler_params=pltpu.CompilerParams(dimension_semantics=("arbitrary",), vmem_limit_bytes=VMEM_LIMIT),
        name="mixer",
    )(h, *per_layer, wsel, lev)


def _router_kernel(h_ref, g_ref, wr_ref, br_ref, xn_ref, meta_ref, code_ref, cnt_ref, carry_ref):
    tm = h_ref.shape[0]

    @pl.when(pl.program_id(0) == 0)
    def _():
        carry_ref[...] = jnp.zeros_like(carry_ref)

    xn = _rms(h_ref[...], g_ref[...])
    xn_ref[...] = xn
    logits = jnp.dot(xn, wr_ref[...], preferred_element_type=F32, precision=lax.Precision.HIGHEST)
    biased = logits + br_ref[...]
    lane = lax.broadcasted_iota(I32, (tm, ROUTER_LANES), 1).astype(F32)
    ninf = -jnp.inf
    big = float(ROUTER_LANES)

    def first_argmax(v):
        m = jnp.max(v, axis=-1, keepdims=True)
        return jnp.min(jnp.where(v == m, lane, big), axis=-1, keepdims=True)

    def pick(v, idx):
        return jnp.sum(jnp.where(lane == idx, v, 0.0), axis=-1, keepdims=True)

    is_g = lane < float(N_GROUPS)
    g_sel = first_argmax(jnp.where(is_g, biased, ninf))
    lg = jnp.where(is_g, logits, ninf)
    eg = jnp.exp(lg - jnp.max(lg, axis=-1, keepdims=True))
    g_w = pick(eg, g_sel) / jnp.sum(eg, axis=-1, keepdims=True)

    lo = float(EXPERT_LANE0) + float(EXP_PER_GROUP) * g_sel
    is_e = (lane >= lo) & (lane < lo + float(EXP_PER_GROUP))
    eb = jnp.where(is_e, biased, ninf)
    i1 = first_argmax(eb)
    i2 = first_argmax(jnp.where(lane == i1, ninf, eb))
    le = jnp.where(is_e, logits, ninf)
    ee = jnp.exp(le - jnp.max(le, axis=-1, keepdims=True))
    den = jnp.sum(ee, axis=-1, keepdims=True)
    p1 = pick(ee, i1) / den
    p2 = pick(ee, i2) / den
    w1 = g_w * (p1 / (p1 + p2))
    w2 = g_w * (p2 / (p1 + p2))

    hot1 = lane == i1
    hot2 = lane == i2
    cnt = jnp.where(hot1 | hot2, 1.0, 0.0)
    row = lax.broadcasted_iota(I32, (tm, tm), 0)
    col = lax.broadcasted_iota(I32, (tm, tm), 1)
    ltri = jnp.where(col < row, 1.0, 0.0).astype(BF16)
    pos = _dot(ltri, cnt.astype(BF16)) + carry_ref[...]
    r1 = jnp.sum(jnp.where(hot1, pos, 0.0), axis=-1, keepdims=True)
    r2 = jnp.sum(jnp.where(hot2, pos, 0.0), axis=-1, keepdims=True)
    carry = carry_ref[...] + jnp.sum(cnt, axis=0, keepdims=True)
    carry_ref[...] = carry
    cnt_ref[...] = carry

    meta_ref[...] = jnp.where(lane == 0.0, w1, jnp.where(lane == 1.0, w2, 0.0))
    scale = float(1 << CODE_SHIFT)
    code1 = (i1 - float(EXPERT_LANE0)) * scale + r1
    code2 = (i2 - float(EXPERT_LANE0)) * scale + r2
    codes = jnp.where(lane == 0.0, code1, jnp.where(lane == 1.0, code2, 0.0))
    code_ref[...] = jnp.transpose(codes)[:CODE_ROWS, :].astype(I32)


def _router(layer, h, g, wr, br):
    t, d = h.shape
    tm = min(TM_ROUTE, t)
    return pl.pallas_call(
        _router_kernel,
        grid=(t // tm,),
        in_specs=[pl.BlockSpec((tm, d), lambda i: (i, 0))] + [_layer_spec(a, layer, 1) for a in (g, wr, br)],
        out_specs=[
            pl.BlockSpec((tm, d), lambda i: (i, 0)),
            pl.BlockSpec((tm, ROUTER_LANES), lambda i: (i, 0)),
            pl.BlockSpec((CODE_ROWS, tm), lambda i: (0, i)),
            pl.BlockSpec((1, ROUTER_LANES), lambda i: (0, 0)),
        ],
        out_shape=[
            jax.ShapeDtypeStruct((t, d), F32),
            jax.ShapeDtypeStruct((t, ROUTER_LANES), F32),
            jax.ShapeDtypeStruct((CODE_ROWS, t), I32),
            jax.ShapeDtypeStruct((1, ROUTER_LANES), F32),
        ],
        scratch_shapes=[pltpu.VMEM((1, ROUTER_LANES), F32)],
        compiler_params=pltpu.CompilerParams(dimension_semantics=("arbitrary",), vmem_limit_bytes=VMEM_LIMIT),
        name="router",
    )(h, g, wr, br)


def _slot_row(code, pstart_ref):
    return pstart_ref[code >> CODE_SHIFT] + (code & ((1 << CODE_SHIFT) - 1))


def _invert_kernel(c1_ref, c2_ref, ps_ref, pad0_ref, pad1_ref, tok_ref):
    def body(t, carry):
        tok_ref[_slot_row(c1_ref[t], ps_ref)] = t
        tok_ref[_slot_row(c2_ref[t], ps_ref)] = t
        return carry

    lax.fori_loop(0, c1_ref.shape[0], body, 0)

    def fill(g, carry):
        def one(r, c2):
            tok_ref[r] = 0
            return c2
        lax.fori_loop(pad0_ref[g], pad1_ref[g], one, 0)
        return carry

    lax.fori_loop(0, pad0_ref.shape[0], fill, 0)


def _invert(c1, c2, pstart, pad0, pad1, cap):
    smem = pl.BlockSpec(memory_space=pltpu.SMEM)
    return pl.pallas_call(
        _invert_kernel,
        in_specs=[smem] * 5,
        out_specs=smem,
        out_shape=jax.ShapeDtypeStruct((cap,), I32),
        name="invert",
    )(c1, c2, pstart, pad0, pad1)


def _expert_kernel(be_ref, nv_ref, nu_ref, tok_ref, xn_hbm, wg_ref, wu_ref, wd_ref, y_ref,
                   xbuf, wgb, wub, wdb, sem):
    b = pl.program_id(0)
    nu = nu_ref[0]
    bm = xbuf.shape[1]

    def row_copy(tok, slot, r):
        return pltpu.make_async_copy(xn_hbm.at[pl.ds(tok, 1)], xbuf.at[slot, pl.ds(r, 1)], sem.at[slot])

    def start_gather(blk):
        slot = blk % 2

        def body(r, carry):
            row_copy(tok_ref[blk * bm + r], slot, r).start()
            return carry

        lax.fori_loop(0, nv_ref[blk], body, 0)

    def wait_gather(blk):
        slot = blk % 2

        def body(r, carry):
            row_copy(0, slot, r).wait()
            return carry

        lax.fori_loop(0, nv_ref[blk], body, 0)

    @pl.when(b == 0)
    def _():
        xbuf[...] = jnp.zeros_like(xbuf)
        start_gather(0)

    @pl.when(b + 1 < nu)
    def _():
        start_gather(b + 1)

    @pl.when(b < nu)
    def _():
        e = be_ref[b]
        changed = (b == 0) | (e != be_ref[jnp.maximum(b - 1, 0)])

        @pl.when(changed)
        def _():
            wgb[...] = wg_ref[...].astype(BF16)
            wub[...] = wu_ref[...].astype(BF16)
            wdb[...] = wd_ref[...].astype(BF16)

        wait_gather(b)
        x = xbuf[b % 2].astype(BF16)
        a = _dot(x, wgb[...])
        u = _dot(x, wub[...])
        hmid = (a * _sigmoid(a)) * u
        y_ref[...] = _dot(hmid.astype(BF16), wdb[...])

    @pl.when(b >= nu)
    def _():
        y_ref[...] = jnp.zeros_like(y_ref)


def _experts(layer, block_e, n_valid, n_used, tok, xn, wg, wu, wd):
    t, d = xn.shape
    de = wg.shape[-1]
    nb = block_e.shape[0]
    wmap = lambda b, be, nv, nu, tk: (layer, be[b], 0, 0)
    return pl.pallas_call(
        _expert_kernel,
        grid_spec=pltpu.PrefetchScalarGridSpec(
            num_scalar_prefetch=4,
            grid=(nb,),
            in_specs=[
                pl.BlockSpec(memory_space=pl.ANY),
                pl.BlockSpec((None, None, d, de), wmap),
                pl.BlockSpec((None, None, d, de), wmap),
                pl.BlockSpec((None, None, de, d), wmap),
            ],
            out_specs=pl.BlockSpec((BM_EXP, d), lambda b, be, nv, nu, tk: (b, 0)),
            scratch_shapes=[
                pltpu.VMEM((2, BM_EXP, d), F32),
                pltpu.VMEM((d, de), BF16), pltpu.VMEM((d, de), BF16), pltpu.VMEM((de, d), BF16),
                pltpu.SemaphoreType.DMA((2,)),
            ],
        ),
        out_shape=jax.ShapeDtypeStruct((nb * BM_EXP, d), F32),
        compiler_params=pltpu.CompilerParams(dimension_semantics=("arbitrary",), vmem_limit_bytes=VMEM_LIMIT),
        name="experts",
    )(block_e, n_valid, n_used, tok, xn, wg, wu, wd)


def _combine_kernel(c1_ref, c2_ref, ps_ref, h_ref, meta_ref, p_ref, wpe_ref, npe_ref, npg_ref, wpg_ref, nf_ref,
                    y_hbm, o_ref, y1buf, y2buf, sem, *, final):
    tm = h_ref.shape[0]
    base = pl.program_id(0) * tm

    def row_copy(src_row, buf, r):
        return pltpu.make_async_copy(y_hbm.at[pl.ds(src_row, 1)], buf.at[pl.ds(r, 1)], sem)

    def issue(r, carry):
        row_copy(_slot_row(c1_ref[base + r], ps_ref), y1buf, r).start()
        row_copy(_slot_row(c2_ref[base + r], ps_ref), y2buf, r).start()
        return carry

    lax.fori_loop(0, tm, issue, 0)

    emb = _rms(_dot(p_ref[...].astype(BF16), wpe_ref[...]), npe_ref[...])

    def drain(r, carry):
        row_copy(0, y1buf, r).wait()
        row_copy(0, y2buf, r).wait()
        return carry

    lax.fori_loop(0, tm, drain, 0)

    meta = meta_ref[...]
    h2 = h_ref[...] + (y1buf[...] * meta[:, 0:1] + y2buf[...] * meta[:, 1:2])
    gate = _sigmoid(_dot(_rms(h2, npg_ref[...]).astype(BF16), wpg_ref[...]))
    out = h2 + gate * emb
    if final:
        out = _rms(out, nf_ref[...])
    o_ref[...] = out


def _combine(layer, c1, c2, pstart, h, meta, p, wpe, npe, npg, wpg, nf, y, final):
    t, d = h.shape
    tm = min(TM_COMB, t)
    tile = lambda w: pl.BlockSpec((tm, w), lambda i, *s: (i, 0))
    return pl.pallas_call(
        functools.partial(_combine_kernel, final=final),
        grid_spec=pltpu.PrefetchScalarGridSpec(
            num_scalar_prefetch=3,
            grid=(t // tm,),
            in_specs=[
                tile(d), tile(ROUTER_LANES),
                pl.BlockSpec((None, None, tm, p.shape[-1]), lambda i, *s: (layer, 0, i, 0)),
                _layer_spec(wpe, layer, 4), _layer_spec(npe, layer, 4), _layer_spec(npg, layer, 4),
                _layer_spec(wpg, layer, 4),
                pl.BlockSpec(nf.shape, lambda i, *s: (0, 0)),
                pl.BlockSpec(memory_space=pl.ANY),
            ],
            out_specs=tile(d),
            scratch_shapes=[pltpu.VMEM((tm, d), F32), pltpu.VMEM((tm, d), F32), pltpu.SemaphoreType.DMA],
        ),
        out_shape=jax.ShapeDtypeStruct((t, d), F32),
        compiler_params=pltpu.CompilerParams(dimension_semantics=("arbitrary",), vmem_limit_bytes=VMEM_LIMIT),
        name="combine",
    )(c1, c2, pstart, h, meta, p, wpe, npe, npg, wpg, nf, y)


def _routing_tables(cnt, t):
    counts = cnt[0, EXPERT_LANE0:EXPERT_LANE0 + N_EXPERTS].astype(I32)
    padded = (counts + BM_EXP - 1) // BM_EXP * BM_EXP
    pend = jnp.cumsum(padded)
    pstart = pend - padded
    nb = -(-2 * t // BM_EXP) + N_EXPERTS
    bstart = jnp.arange(nb, dtype=I32) * BM_EXP
    block_e = jnp.minimum(jnp.sum((pend[None, :] <= bstart[:, None]).astype(I32), axis=1), N_EXPERTS - 1)
    n_valid = jnp.clip((pstart + counts)[block_e] - bstart, 0, BM_EXP).astype(I32)
    n_used = (pend[-1:] // BM_EXP).astype(I32)
    cap = nb * BM_EXP
    pad0 = jnp.concatenate([pstart + counts, pend[-1:]]).astype(I32)
    pad1 = jnp.concatenate([pend, jnp.full((1,), cap, I32)]).astype(I32)
    return pstart.astype(I32), block_e.astype(I32), n_valid, n_used, pad0, pad1, cap


def kernel(x, p, norm_mix, w_in, lb_logits, rec_out_gain, sgu_ln_g, sgu_ln_b, sgu_w, sgu_b, w_out, norm_ffn, w_rg, b_rg, w_re, b_re, w_gate, w_up, w_down, norm_pg, w_pg, w_pe, norm_pe, norm_f):
    bsz, seq, d = x.shape
    depth = w_in.shape[0]
    assert bsz == 1
    h = x.reshape(seq, d)

    lb = jnp.cumsum(jax.nn.softmax(lb_logits.astype(F32), axis=0), axis=0)
    lb = lb - lb[0]
    rows = lambda v: v.reshape(depth, 1, -1)
    la, l1, oml = rows(jnp.log(lb)), rows(jnp.log1p(-lb)), rows(1.0 - lb)
    gain = rows(jnp.tile(rec_out_gain, (1, H_REC)))
    sb = jnp.repeat(jnp.swapaxes(sgu_b, 1, 2), D_SGU // H_SGU, axis=2)
    win_b, wout_b, wpg_b, wpe_b = (w.astype(BF16) for w in (w_in, w_out, w_pg, w_pe))
    npad = ROUTER_LANES - N_GROUPS - N_EXPERTS
    wr = jnp.concatenate([w_rg, w_re, jnp.zeros((depth, d, npad), F32)], axis=2)
    br = rows(jnp.concatenate([b_rg, b_re, jnp.zeros((depth, npad), F32)], axis=1))
    wsel = jnp.asarray(_decay_selectors(), BF16)
    lev = jnp.asarray(_pair_levels())
    gmix, gffn, gpg, gpe, lng, lnb = (rows(v) for v in (norm_mix, norm_ffn, norm_pg, norm_pe, sgu_ln_g, sgu_ln_b))
    nf = norm_f.reshape(1, d)

    for l in range(depth):
        h = _mixer(l, h, gmix, win_b, la, l1, oml, gain, lng, lnb, sgu_w, sb, wout_b, wsel, lev)
        xn, meta, codes, cnt = _router(l, h, gffn, wr, br)
        pstart, block_e, n_valid, n_used, pad0, pad1, cap = _routing_tables(cnt, seq)
        c1, c2 = codes[0], codes[1]
        tok = _invert(c1, c2, pstart, pad0, pad1, cap)
        y = _experts(l, block_e, n_valid, n_used, tok, xn, w_gate, w_up, w_down)
        h = _combine(l, c1, c2, pstart, h, meta, p, wpe_b, gpe, gpg, wpg_b, nf, y, final=(l == depth - 1))
    return h.reshape(bsz, seq, d)
```

```python
import functools

import numpy as np
import jax
import jax.numpy as jnp
from jax import lax
from jax.experimental import pallas as pl
from jax.experimental.pallas import tpu as pltpu

F32 = jnp.float32
BF16 = jnp.bfloat16
I32 = jnp.int32

EPS = 1e-6
D_REC = 512
D_SGU = 512
H_REC = 4
DK_REC = 128
CHUNK_REC = 64
SGU_CHUNK = 128
H_SGU = 4
N_GROUPS = 4
EXP_PER_GROUP = 8
N_EXPERTS = N_GROUPS * EXP_PER_GROUP
ROUTER_LANES = 128
EXPERT_LANE0 = N_GROUPS
CODE_SHIFT = 16
CODE_ROWS = 8

TM_MIX = 256
TM_ROUTE = 512
TM_COMB = 256
BM_EXP = 256
DMA_UNROLL = 8
VMEM_LIMIT = 48 * 1024 * 1024

_LEVELS = (2, 4, 8, 16, 32, 64)


def _decay_selectors():
    t = np.arange(CHUNK_REC)[:, None]
    s = np.arange(CHUNK_REC)[None, :]
    mats = [((t // m == s // m) & (s <= t)) for m in _LEVELS]
    mats += [((t // m == s // m) & (s > t)) for m in _LEVELS]
    return np.concatenate(mats, axis=0).astype(np.float32)


def _pair_levels():
    t = np.arange(CHUNK_REC)[:, None]
    s = np.arange(CHUNK_REC)[None, :]
    x = t ^ s
    lev = np.floor(np.log2(np.maximum(x, 1))).astype(np.int32)
    lev = np.where(t == s, -1, lev)
    lev = np.where(s > t, -2, lev)
    return lev.astype(np.int32)


def _rms(x, g):
    return x * lax.rsqrt(jnp.mean(x * x, axis=-1, keepdims=True) + EPS) * g


def _sigmoid(x):
    return 1.0 / (1.0 + jnp.exp(-x))


def _gelu_tanh(x):
    return x * (0.5 * (1.0 + jnp.tanh(0.7978845608028654 * (x + 0.044715 * (x * x * x)))))


def _dot(a, b):
    return jnp.dot(a, b, preferred_element_type=F32)


def _dot_nt(a, b):
    return lax.dot_general(a, b, (((1,), (1,)), ((), ())), preferred_element_type=F32)


def _dot_tn(a, b):
    return lax.dot_general(a, b, (((0,), (0,)), ((), ())), preferred_element_type=F32)


def _layer_spec(arr, layer, nargs):
    zeros = (0,) * (arr.ndim - 1)
    return pl.BlockSpec((None,) + arr.shape[1:], lambda *a: (layer,) + zeros)


def _mixer_kernel(h_ref, gmix_ref, win_ref, la_ref, l1_ref, oml_ref, gain_ref, lng_ref, lnb_ref,
                  ws_ref, sb_ref, wout_ref, wsel_ref, lev_ref, o_ref, st_ref, ocat_ref):
    tm = h_ref.shape[0]

    @pl.when(pl.program_id(0) == 0)
    def _():
        st_ref[...] = jnp.zeros_like(st_ref)

    h = h_ref[...]
    hb = _rms(h, gmix_ref[...]).astype(BF16)

    def proj(j):
        return _dot(hb, win_ref[:, j * 512:(j + 1) * 512])

    zq = proj(0)
    q = zq * _sigmoid(zq)
    fz = proj(1)
    iv = proj(2).astype(BF16)
    zg = proj(3)
    gate = zg * _sigmoid(zg)

    ls = jnp.minimum(fz, 0.0) - jnp.log1p(jnp.exp(-jnp.abs(fz)))
    c = l1_ref[...] + ls
    la = la_ref[...]
    lf = jnp.maximum(la, c) + jnp.log1p(jnp.exp(-jnp.abs(la - c)))
    kk = oml_ref[...] * (1.0 / (1.0 + jnp.exp(fz)))

    wsel = wsel_ref[...]
    lev = lev_ref[...]
    gain = gain_ref[...]
    nlev = len(_LEVELS)

    for cidx in range(tm // CHUNK_REC):
        sl = slice(cidx * CHUNK_REC, (cidx + 1) * CHUNK_REC)
        lf_c = lf[sl]
        hi = lf_c.astype(BF16)
        mid = (lf_c - hi.astype(F32)).astype(BF16)
        xs = _dot(wsel, jnp.concatenate([hi, mid], axis=1))
        pw = jnp.exp(xs[:, :D_REC] + xs[:, D_REC:])
        f_c = jnp.exp(lf_c)
        for hh in range(H_REC):
            hs = slice(hh * DK_REC, (hh + 1) * DK_REC)
            q_h = q[sl, hs]
            k_h = kk[sl, hs]
            i_h = iv[sl, hs]
            k_b = k_h.astype(BF16)
            sc = jnp.where(lev == -1, _dot_nt(q_h.astype(BF16), k_b), 0.0)
            sc = sc + jnp.where(lev == 0, _dot_nt((q_h * f_c[:, hs]).astype(BF16), k_b), 0.0)
            for j in range(nlev - 1):
                qd = (q_h * pw[j * 64:(j + 1) * 64, hs]).astype(BF16)
                kd = (k_h * pw[(nlev + j) * 64:(nlev + j + 1) * 64, hs]).astype(BF16)
                sc = sc + jnp.where(lev == j + 1, _dot_nt(qd, kd), 0.0)
            jl = nlev - 1
            qb = (q_h * pw[jl * 64:(jl + 1) * 64, hs]).astype(BF16)
            kd = (k_h * pw[(nlev + jl) * 64:(nlev + jl + 1) * 64, hs]).astype(BF16)
            st = st_ref[hh]
            o = _dot(sc.astype(BF16), i_h) + _dot_nt(qb, st.astype(BF16))
            dec = pw[jl * 64 + 63:jl * 64 + 64, hs]
            st_ref[hh] = st * dec + _dot_tn(i_h, kd)
            o = o * lax.rsqrt(jnp.mean(o * o, axis=-1, keepdims=True) + EPS) * gain[:, hs]
            ocat_ref[sl, hs] = (o * gate[sl, hs]).astype(BF16)

    zu = proj(4)
    zv = proj(5)
    r128 = lax.broadcasted_iota(I32, (SGU_CHUNK, SGU_CHUNK), 0)
    c128 = lax.broadcasted_iota(I32, (SGU_CHUNK, SGU_CHUNK), 1)
    wtri = [jnp.where(c128 <= r128, ws_ref[hh], 0.0).astype(BF16) for hh in range(H_SGU)]
    for cidx in range(tm // SGU_CHUNK):
        sl = slice(cidx * SGU_CHUNK, (cidx + 1) * SGU_CHUNK)
        ua = _gelu_tanh(zu[sl])
        va = _gelu_tanh(zv[sl])
        mu = jnp.mean(va, axis=-1, keepdims=True)
        vc = va - mu
        var = jnp.mean(vc * vc, axis=-1, keepdims=True)
        vn = (vc * lax.rsqrt(var + EPS) * lng_ref[...] + lnb_ref[...]).astype(BF16)
        for hh in range(H_SGU):
            hs = slice(hh * 128, (hh + 1) * 128)
            s = _dot(wtri[hh], vn[:, hs]) + sb_ref[:, hs]
            ocat_ref[sl, D_REC + hh * 128:D_REC + (hh + 1) * 128] = (ua[:, hs] * s).astype(BF16)

    o_ref[...] = h + _dot(ocat_ref[...], wout_ref[...])


def _mixer(layer, h, gmix, win, la, l1, oml, gain, lng, lnb, ws, sb, wout, wsel, lev):
    t, d = h.shape
    tm = min(TM_MIX, t)
    per_layer = (gmix, win, la, l1, oml, gain, lng, lnb, ws, sb, wout)
    return pl.pallas_call(
        _mixer_kernel,
        grid=(t // tm,),
        in_specs=[pl.BlockSpec((tm, d), lambda i: (i, 0))]
        + [_layer_spec(a, layer, 1) for a in per_layer]
        + [pl.BlockSpec(wsel.shape, lambda i: (0, 0)), pl.BlockSpec(lev.shape, lambda i: (0, 0))],
        out_specs=pl.BlockSpec((tm, d), lambda i: (i, 0)),
        out_shape=jax.ShapeDtypeStruct((t, d), F32),
        scratch_shapes=[pltpu.VMEM((H_REC, DK_REC, DK_REC), F32), pltpu.VMEM((tm, d), BF16)],
        compiler_params=pltpu.CompilerParams(dimension_semantics=("arbitrary",), vmem_limit_bytes=VMEM_LIMIT),
        name="mixer",
    )(h, *per_layer, wsel, lev)


def _router_kernel(h_ref, g_ref, wr_ref, br_ref, xn_ref, meta_ref, code_ref, cnt_ref, carry_ref):
    tm = h_ref.shape[0]

    @pl.when(pl.program_id(0) == 0)
    def _():
        carry_ref[...] = jnp.zeros_like(carry_ref)

    xn = _rms(h_ref[...], g_ref[...])
    xn_ref[...] = xn
    logits = jnp.dot(xn, wr_ref[...], preferred_element_type=F32, precision=lax.Precision.HIGHEST)
    biased = logits + br_ref[...]
    lane = lax.broadcasted_iota(I32, (tm, ROUTER_LANES), 1).astype(F32)
    ninf = -jnp.inf
    big = float(ROUTER_LANES)

    def first_argmax(v):
        m = jnp.max(v, axis=-1, keepdims=True)
        return jnp.min(jnp.where(v == m, lane, big), axis=-1, keepdims=True)

    def pick(v, idx):
        return jnp.sum(jnp.where(lane == idx, v, 0.0), axis=-1, keepdims=True)

    is_g = lane < float(N_GROUPS)
    g_sel = first_argmax(jnp.where(is_g, biased, ninf))
    lg = jnp.where(is_g, logits, ninf)
    eg = jnp.exp(lg - jnp.max(lg, axis=-1, keepdims=True))
    g_w = pick(eg, g_sel) / jnp.sum(eg, axis=-1, keepdims=True)

    lo = float(EXPERT_LANE0) + float(EXP_PER_GROUP) * g_sel
    is_e = (lane >= lo) & (lane < lo + float(EXP_PER_GROUP))
    eb = jnp.where(is_e, biased, ninf)
    i1 = first_argmax(eb)
    i2 = first_argmax(jnp.where(lane == i1, ninf, eb))
    le = jnp.where(is_e, logits, ninf)
    ee = jnp.exp(le - jnp.max(le, axis=-1, keepdims=True))
    den = jnp.sum(ee, axis=-1, keepdims=True)
    p1 = pick(ee, i1) / den
    p2 = pick(ee, i2) / den
    w1 = g_w * (p1 / (p1 + p2))
    w2 = g_w * (p2 / (p1 + p2))

    hot1 = lane == i1
    hot2 = lane == i2
    cnt = jnp.where(hot1 | hot2, 1.0, 0.0)
    row = lax.broadcasted_iota(I32, (tm, tm), 0)
    col = lax.broadcasted_iota(I32, (tm, tm), 1)
    ltri = jnp.where(col < row, 1.0, 0.0).astype(BF16)
    pos = _dot(ltri, cnt.astype(BF16)) + carry_ref[...]
    r1 = jnp.sum(jnp.where(hot1, pos, 0.0), axis=-1, keepdims=True)
    r2 = jnp.sum(jnp.where(hot2, pos, 0.0), axis=-1, keepdims=True)
    carry = carry_ref[...] + jnp.sum(cnt, axis=0, keepdims=True)
    carry_ref[...] = carry
    cnt_ref[...] = carry

    meta_ref[...] = jnp.where(lane == 0.0, w1, jnp.where(lane == 1.0, w2, 0.0))
    scale = float(1 << CODE_SHIFT)
    code1 = (i1 - float(EXPERT_LANE0)) * scale + r1
    code2 = (i2 - float(EXPERT_LANE0)) * scale + r2
    codes = jnp.where(lane == 0.0, code1, jnp.where(lane == 1.0, code2, 0.0))
    code_ref[...] = jnp.transpose(codes)[:CODE_ROWS, :].astype(I32)


def _router(layer, h, g, wr, br):
    t, d = h.shape
    tm = min(TM_ROUTE, t)
    return pl.pallas_call(
        _router_kernel,
        grid=(t // tm,),
        in_specs=[pl.BlockSpec((tm, d), lambda i: (i, 0))] + [_layer_spec(a, layer, 1) for a in (g, wr, br)],
        out_specs=[
            pl.BlockSpec((tm, d), lambda i: (i, 0)),
            pl.BlockSpec((tm, ROUTER_LANES), lambda i: (i, 0)),
            pl.BlockSpec((CODE_ROWS, tm), lambda i: (0, i)),
            pl.BlockSpec((1, ROUTER_LANES), lambda i: (0, 0)),
        ],
        out_shape=[
            jax.ShapeDtypeStruct((t, d), F32),
            jax.ShapeDtypeStruct((t, ROUTER_LANES), F32),
            jax.ShapeDtypeStruct((CODE_ROWS, t), I32),
            jax.ShapeDtypeStruct((1, ROUTER_LANES), F32),
        ],
        scratch_shapes=[pltpu.VMEM((1, ROUTER_LANES), F32)],
        compiler_params=pltpu.CompilerParams(dimension_semantics=("arbitrary",), vmem_limit_bytes=VMEM_LIMIT),
        name="router",
    )(h, g, wr, br)


def _slot_row(code, pstart_ref):
    return pstart_ref[code >> CODE_SHIFT] + (code & ((1 << CODE_SHIFT) - 1))


def _invert_kernel(c1_ref, c2_ref, ps_ref, pad0_ref, pad1_ref, tok_ref):
    def body(t, carry):
        tok_ref[_slot_row(c1_ref[t], ps_ref)] = t
        tok_ref[_slot_row(c2_ref[t], ps_ref)] = t
        return carry

    lax.fori_loop(0, c1_ref.shape[0], body, 0, unroll=DMA_UNROLL)

    def fill(g, carry):
        def one(r, c2):
            tok_ref[r] = 0
            return c2
        lax.fori_loop(pad0_ref[g], pad1_ref[g], one, 0)
        return carry

    lax.fori_loop(0, pad0_ref.shape[0], fill, 0)


def _invert(c1, c2, pstart, pad0, pad1, cap):
    smem = pl.BlockSpec(memory_space=pltpu.SMEM)
    return pl.pallas_call(
        _invert_kernel,
        in_specs=[smem] * 5,
        out_specs=smem,
        out_shape=jax.ShapeDtypeStruct((cap,), I32),
        name="invert",
    )(c1, c2, pstart, pad0, pad1)


def _expert_kernel(be_ref, nu_ref, tok_ref, xn_hbm, wg_ref, wu_ref, wd_ref, y_ref,
                   xbuf, wgb, wub, wdb, sem):
    b = pl.program_id(0)
    nu = nu_ref[0]
    bm = xbuf.shape[1]

    def start_gather(blk):
        slot = blk % 2

        def body(r, carry):
            src = xn_hbm.at[pl.ds(tok_ref[blk * bm + r], 1)]
            pltpu.make_async_copy(src, xbuf.at[slot, pl.ds(r, 1)], sem.at[slot]).start()
            return carry

        lax.fori_loop(0, bm, body, 0, unroll=DMA_UNROLL)

    def wait_gather(blk):
        slot = blk % 2
        pltpu.make_async_copy(xn_hbm.at[pl.ds(0, bm)], xbuf.at[slot], sem.at[slot]).wait()

    @pl.when(b == 0)
    def _():
        start_gather(0)

    @pl.when(b + 1 < nu)
    def _():
        start_gather(b + 1)

    @pl.when(b < nu)
    def _():
        e = be_ref[b]
        changed = (b == 0) | (e != be_ref[jnp.maximum(b - 1, 0)])

        @pl.when(changed)
        def _():
            wgb[...] = wg_ref[...].astype(BF16)
            wub[...] = wu_ref[...].astype(BF16)
            wdb[...] = wd_ref[...].astype(BF16)

        wait_gather(b)
        x = xbuf[b % 2].astype(BF16)
        a = _dot(x, wgb[...])
        u = _dot(x, wub[...])
        hmid = (a * _sigmoid(a)) * u
        y_ref[...] = _dot(hmid.astype(BF16), wdb[...])

    @pl.when(b >= nu)
    def _():
        y_ref[...] = jnp.zeros_like(y_ref)


def _experts(layer, block_e, n_used, tok, xn, wg, wu, wd):
    t, d = xn.shape
    de = wg.shape[-1]
    nb = block_e.shape[0]
    wmap = lambda b, be, nu, tk: (layer, be[b], 0, 0)
    return pl.pallas_call(
        _expert_kernel,
        grid_spec=pltpu.PrefetchScalarGridSpec(
            num_scalar_prefetch=3,
            grid=(nb,),
            in_specs=[
                pl.BlockSpec(memory_space=pl.ANY),
                pl.BlockSpec((None, None, d, de), wmap),
                pl.BlockSpec((None, None, d, de), wmap),
                pl.BlockSpec((None, None, de, d), wmap),
            ],
            out_specs=pl.BlockSpec((BM_EXP, d), lambda b, be, nu, tk: (b, 0)),
            scratch_shapes=[
                pltpu.VMEM((2, BM_EXP, d), F32),
                pltpu.VMEM((d, de), BF16), pltpu.VMEM((d, de), BF16), pltpu.VMEM((de, d), BF16),
                pltpu.SemaphoreType.DMA((2,)),
            ],
        ),
        out_shape=jax.ShapeDtypeStruct((nb * BM_EXP, d), F32),
        compiler_params=pltpu.CompilerParams(dimension_semantics=("arbitrary",), vmem_limit_bytes=VMEM_LIMIT),
        name="experts",
    )(block_e, n_used, tok, xn, wg, wu, wd)


def _combine_kernel(c1_ref, c2_ref, ps_ref, h_ref, meta_ref, p_ref, wpe_ref, npe_ref, npg_ref, wpg_ref, nf_ref,
                    y_hbm, o_ref, ybuf, sem, *, final):
    tm = h_ref.shape[0]
    i = pl.program_id(0)

    def start_gather(tile):
        slot = tile % 2
        base = tile * tm

        def body(r, carry):
            for k, c_ref in enumerate((c1_ref, c2_ref)):
                src = y_hbm.at[pl.ds(_slot_row(c_ref[base + r], ps_ref), 1)]
                pltpu.make_async_copy(src, ybuf.at[slot, k, pl.ds(r, 1)], sem.at[slot]).start()
            return carry

        lax.fori_loop(0, tm, body, 0, unroll=DMA_UNROLL)

    @pl.when(i == 0)
    def _():
        start_gather(0)

    @pl.when(i + 1 < pl.num_programs(0))
    def _():
        start_gather(i + 1)

    emb = _rms(_dot(p_ref[...].astype(BF16), wpe_ref[...]), npe_ref[...])

    slot = i % 2
    for k in range(2):
        pltpu.make_async_copy(y_hbm.at[pl.ds(0, tm)], ybuf.at[slot, k], sem.at[slot]).wait()

    meta = meta_ref[...]
    h2 = h_ref[...] + (ybuf[slot, 0] * meta[:, 0:1] + ybuf[slot, 1] * meta[:, 1:2])
    gate = _sigmoid(_dot(_rms(h2, npg_ref[...]).astype(BF16), wpg_ref[...]))
    out = h2 + gate * emb
    if final:
        out = _rms(out, nf_ref[...])
    o_ref[...] = out


def _combine(layer, c1, c2, pstart, h, meta, p, wpe, npe, npg, wpg, nf, y, final):
    t, d = h.shape
    tm = min(TM_COMB, t)
    tile = lambda w: pl.BlockSpec((tm, w), lambda i, *s: (i, 0))
    return pl.pallas_call(
        functools.partial(_combine_kernel, final=final),
        grid_spec=pltpu.PrefetchScalarGridSpec(
            num_scalar_prefetch=3,
            grid=(t // tm,),
            in_specs=[
                tile(d), tile(ROUTER_LANES),
                pl.BlockSpec((None, None, tm, p.shape[-1]), lambda i, *s: (layer, 0, i, 0)),
                _layer_spec(wpe, layer, 4), _layer_spec(npe, layer, 4), _layer_spec(npg, layer, 4),
                _layer_spec(wpg, layer, 4),
                pl.BlockSpec(nf.shape, lambda i, *s: (0, 0)),
                pl.BlockSpec(memory_space=pl.ANY),
            ],
            out_specs=tile(d),
            scratch_shapes=[pltpu.VMEM((2, 2, tm, d), F32), pltpu.SemaphoreType.DMA((2,))],
        ),
        out_shape=jax.ShapeDtypeStruct((t, d), F32),
        compiler_params=pltpu.CompilerParams(dimension_semantics=("arbitrary",), vmem_limit_bytes=VMEM_LIMIT),
        name="combine",
    )(c1, c2, pstart, h, meta, p, wpe, npe, npg, wpg, nf, y)


def _routing_tables(cnt, t):
    counts = cnt[0, EXPERT_LANE0:EXPERT_LANE0 + N_EXPERTS].astype(I32)
    padded = (counts + BM_EXP - 1) // BM_EXP * BM_EXP
    pend = jnp.cumsum(padded)
    pstart = pend - padded
    nb = -(-2 * t // BM_EXP) + N_EXPERTS
    bstart = jnp.arange(nb, dtype=I32) * BM_EXP
    block_e = jnp.minimum(jnp.sum((pend[None, :] <= bstart[:, None]).astype(I32), axis=1), N_EXPERTS - 1)
    n_used = (pend[-1:] // BM_EXP).astype(I32)
    cap = nb * BM_EXP
    pad0 = jnp.concatenate([pstart + counts, pend[-1:]]).astype(I32)
    pad1 = jnp.concatenate([pend, jnp.full((1,), cap, I32)]).astype(I32)
    return pstart.astype(I32), block_e.astype(I32), n_used, pad0, pad1, cap


def kernel(x, p, norm_mix, w_in, lb_logits, rec_out_gain, sgu_ln_g, sgu_ln_b, sgu_w, sgu_b, w_out, norm_ffn, w_rg, b_rg, w_re, b_re, w_gate, w_up, w_down, norm_pg, w_pg, w_pe, norm_pe, norm_f):
    bsz, seq, d = x.shape
    depth = w_in.shape[0]
    assert bsz == 1
    h = x.reshape(seq, d)

    lb = jnp.cumsum(jax.nn.softmax(lb_logits.astype(F32), axis=0), axis=0)
    lb = lb - lb[0]
    rows = lambda v: v.reshape(depth, 1, -1)
    la, l1, oml = rows(jnp.log(lb)), rows(jnp.log1p(-lb)), rows(1.0 - lb)
    gain = rows(jnp.tile(rec_out_gain, (1, H_REC)))
    sb = jnp.repeat(jnp.swapaxes(sgu_b, 1, 2), D_SGU // H_SGU, axis=2)
    win_b, wout_b, wpg_b, wpe_b = (w.astype(BF16) for w in (w_in, w_out, w_pg, w_pe))
    npad = ROUTER_LANES - N_GROUPS - N_EXPERTS
    wr = jnp.concatenate([w_rg, w_re, jnp.zeros((depth, d, npad), F32)], axis=2)
    br = rows(jnp.concatenate([b_rg, b_re, jnp.zeros((depth, npad), F32)], axis=1))
    wsel = jnp.asarray(_decay_selectors(), BF16)
    lev = jnp.asarray(_pair_levels())
    gmix, gffn, gpg, gpe, lng, lnb = (rows(v) for v in (norm_mix, norm_ffn, norm_pg, norm_pe, sgu_ln_g, sgu_ln_b))
    nf = norm_f.reshape(1, d)

    for l in range(depth):
        h = _mixer(l, h, gmix, win_b, la, l1, oml, gain, lng, lnb, sgu_w, sb, wout_b, wsel, lev)
        xn, meta, codes, cnt = _router(l, h, gffn, wr, br)
        pstart, block_e, n_used, pad0, pad1, cap = _routing_tables(cnt, seq)
        c1, c2 = codes[0], codes[1]
        tok = _invert(c1, c2, pstart, pad0, pad1, cap)
        y = _experts(l, block_e, n_used, tok, xn, w_gate, w_up, w_down)
        h = _combine(l, c1, c2, pstart, h, meta, p, wpe_b, gpe, gpg, wpg_b, nf, y, final=(l == depth - 1))
    return h.reshape(bsz, seq, d)
```

```python
import functools

import numpy as np
import jax
import jax.numpy as jnp
from jax import lax
from jax.experimental import pallas as pl
from jax.experimental.pallas import tpu as pltpu

F32 = jnp.float32
BF16 = jnp.bfloat16
I32 = jnp.int32

EPS = 1e-6
D_REC = 512
D_SGU = 512
H_REC = 4
DK_REC = 128
CHUNK_REC = 64
SGU_CHUNK = 128
H_SGU = 4
N_GROUPS = 4
EXP_PER_GROUP = 8
N_EXPERTS = N_GROUPS * EXP_PER_GROUP
TOP_K = 2
ROUTER_LANES = 128
EXPERT_LANE0 = N_GROUPS
SUBLANES = 8

TM_MIX = 256
TM_MOE = 256
STAGE_ROWS = 768
BM_EXP = 256
VMEM_LIMIT = 48 * 1024 * 1024

assert STAGE_ROWS >= TOP_K * TM_MOE + (SUBLANES - 1) * N_EXPERTS + SUBLANES
ZERO_CHUNK = STAGE_ROWS // SUBLANES - 1

_LEVELS = (2, 4, 8, 16, 32, 64)


def _decay_selectors():
    t = np.arange(CHUNK_REC)[:, None]
    s = np.arange(CHUNK_REC)[None, :]
    mats = [((t // m == s // m) & (s <= t)) for m in _LEVELS]
    mats += [((t // m == s // m) & (s > t)) for m in _LEVELS]
    return np.concatenate(mats, axis=0).astype(np.float32)


def _pair_levels():
    t = np.arange(CHUNK_REC)[:, None]
    s = np.arange(CHUNK_REC)[None, :]
    x = t ^ s
    lev = np.floor(np.log2(np.maximum(x, 1))).astype(np.int32)
    lev = np.where(t == s, -1, lev)
    lev = np.where(s > t, -2, lev)
    return lev.astype(np.int32)


def _rms(x, g):
    return x * lax.rsqrt(jnp.mean(x * x, axis=-1, keepdims=True) + EPS) * g


def _sigmoid(x):
    return 1.0 / (1.0 + jnp.exp(-x))


def _gelu_tanh(x):
    return x * (0.5 * (1.0 + jnp.tanh(0.7978845608028654 * (x + 0.044715 * (x * x * x)))))


def _dot(a, b):
    return jnp.dot(a, b, preferred_element_type=F32)


def _dot_nt(a, b):
    return lax.dot_general(a, b, (((1,), (1,)), ((), ())), preferred_element_type=F32)


def _dot_tn(a, b):
    return lax.dot_general(a, b, (((0,), (0,)), ((), ())), preferred_element_type=F32)


def _layer_spec(arr, layer):
    zeros = (0,) * (arr.ndim - 1)
    return pl.BlockSpec((None,) + arr.shape[1:], lambda *a: (layer,) + zeros)


def _mixer_kernel(h_ref, gmix_ref, win_ref, la_ref, l1_ref, oml_ref, gain_ref, lng_ref, lnb_ref,
                  ws_ref, sb_ref, wout_ref, wsel_ref, lev_ref, o_ref, st_ref, ocat_ref):
    tm = h_ref.shape[0]

    @pl.when(pl.program_id(0) == 0)
    def _():
        st_ref[...] = jnp.zeros_like(st_ref)

    h = h_ref[...]
    hb = _rms(h, gmix_ref[...]).astype(BF16)

    def proj(j):
        return _dot(hb, win_ref[:, j * 512:(j + 1) * 512])

    zq = proj(0)
    q = zq * _sigmoid(zq)
    fz = proj(1)
    iv = proj(2).astype(BF16)
    zg = proj(3)
    gate = zg * _sigmoid(zg)

    ls = jnp.minimum(fz, 0.0) - jnp.log1p(jnp.exp(-jnp.abs(fz)))
    c = l1_ref[...] + ls
    la = la_ref[...]
    lf = jnp.maximum(la, c) + jnp.log1p(jnp.exp(-jnp.abs(la - c)))
    kk = oml_ref[...] * (1.0 / (1.0 + jnp.exp(fz)))

    wsel = wsel_ref[...]
    lev = lev_ref[...]
    gain = gain_ref[...]
    nlev = len(_LEVELS)

    for cidx in range(tm // CHUNK_REC):
        sl = slice(cidx * CHUNK_REC, (cidx + 1) * CHUNK_REC)
        lf_c = lf[sl]
        hi = lf_c.astype(BF16)
        mid = (lf_c - hi.astype(F32)).astype(BF16)
        xs = _dot(wsel, jnp.concatenate([hi, mid], axis=1))
        pw = jnp.exp(xs[:, :D_REC] + xs[:, D_REC:])
        f_c = jnp.exp(lf_c)
        for hh in range(H_REC):
            hs = slice(hh * DK_REC, (hh + 1) * DK_REC)
            q_h = q[sl, hs]
            k_h = kk[sl, hs]
            i_h = iv[sl, hs]
            k_b = k_h.astype(BF16)
            sc = jnp.where(lev == -1, _dot_nt(q_h.astype(BF16), k_b), 0.0)
            sc = sc + jnp.where(lev == 0, _dot_nt((q_h * f_c[:, hs]).astype(BF16), k_b), 0.0)
            for j in range(nlev - 1):
                qd = (q_h * pw[j * 64:(j + 1) * 64, hs]).astype(BF16)
                kd = (k_h * pw[(nlev + j) * 64:(nlev + j + 1) * 64, hs]).astype(BF16)
                sc = sc + jnp.where(lev == j + 1, _dot_nt(qd, kd), 0.0)
            jl = nlev - 1
            qb = (q_h * pw[jl * 64:(jl + 1) * 64, hs]).astype(BF16)
            kd = (k_h * pw[(nlev + jl) * 64:(nlev + jl + 1) * 64, hs]).astype(BF16)
            st = st_ref[hh]
            o = _dot(sc.astype(BF16), i_h) + _dot_nt(qb, st.astype(BF16))
            dec = pw[jl * 64 + 63:jl * 64 + 64, hs]
            st_ref[hh] = st * dec + _dot_tn(i_h, kd)
            o = o * lax.rsqrt(jnp.mean(o * o, axis=-1, keepdims=True) + EPS) * gain[:, hs]
            ocat_ref[sl, hs] = (o * gate[sl, hs]).astype(BF16)

    zu = proj(4)
    zv = proj(5)
    r128 = lax.broadcasted_iota(I32, (SGU_CHUNK, SGU_CHUNK), 0)
    c128 = lax.broadcasted_iota(I32, (SGU_CHUNK, SGU_CHUNK), 1)
    wtri = [jnp.where(c128 <= r128, ws_ref[hh], 0.0).astype(BF16) for hh in range(H_SGU)]
    for cidx in range(tm // SGU_CHUNK):
        sl = slice(cidx * SGU_CHUNK, (cidx + 1) * SGU_CHUNK)
        ua = _gelu_tanh(zu[sl])
        va = _gelu_tanh(zv[sl])
        mu = jnp.mean(va, axis=-1, keepdims=True)
        vc = va - mu
        var = jnp.mean(vc * vc, axis=-1, keepdims=True)
        vn = (vc * lax.rsqrt(var + EPS) * lng_ref[...] + lnb_ref[...]).astype(BF16)
        for hh in range(H_SGU):
            hs = slice(hh * 128, (hh + 1) * 128)
            s = _dot(wtri[hh], vn[:, hs]) + sb_ref[:, hs]
            ocat_ref[sl, D_REC + hh * 128:D_REC + (hh + 1) * 128] = (ua[:, hs] * s).astype(BF16)

    o_ref[...] = h + _dot(ocat_ref[...], wout_ref[...])


def _mixer(layer, h, gmix, win, la, l1, oml, gain, lng, lnb, ws, sb, wout, wsel, lev):
    t, d = h.shape
    tm = min(TM_MIX, t)
    per_layer = (gmix, win, la, l1, oml, gain, lng, lnb, ws, sb, wout)
    return pl.pallas_call(
        _mixer_kernel,
        grid=(t // tm,),
        in_specs=[pl.BlockSpec((tm, d), lambda i: (i, 0))]
        + [_layer_spec(a, layer) for a in per_layer]
        + [pl.BlockSpec(wsel.shape, lambda i: (0, 0)), pl.BlockSpec(lev.shape, lambda i: (0, 0))],
        out_specs=pl.BlockSpec((tm, d), lambda i: (i, 0)),
        out_shape=jax.ShapeDtypeStruct((t, d), F32),
        scratch_shapes=[pltpu.VMEM((H_REC, DK_REC, DK_REC), F32), pltpu.VMEM((tm, d), BF16)],
        compiler_params=pltpu.CompilerParams(dimension_semantics=("arbitrary",), vmem_limit_bytes=VMEM_LIMIT),
        name="mixer",
    )(h, *per_layer, wsel, lev)


def _stage_selectors(meta, so_row):
    tm = meta.shape[0]
    lane = lax.broadcasted_iota(I32, (tm, ROUTER_LANES), 1).astype(F32)
    col = lax.broadcasted_iota(I32, (tm, STAGE_ROWS), 1)
    sels = []
    for k in range(TOP_K):
        run0 = jnp.sum(jnp.where(lane == meta[:, 2 + k:3 + k], so_row, 0.0), axis=-1, keepdims=True)
        pos = (run0 + meta[:, 4 + k:5 + k]).astype(I32)
        sels.append(col == pos)
    return sels


def _router_kernel(h_ref, g_ref, wr_ref, br_ref, xs_ref, meta_ref, tab_ref):
    tm = h_ref.shape[0]
    xn = _rms(h_ref[...], g_ref[...])
    logits = jnp.dot(xn, wr_ref[...], preferred_element_type=F32, precision=lax.Precision.HIGHEST)
    biased = logits + br_ref[...]
    lane = lax.broadcasted_iota(I32, (tm, ROUTER_LANES), 1).astype(F32)
    ninf = -jnp.inf
    big = float(ROUTER_LANES)

    def first_argmax(v):
        m = jnp.max(v, axis=-1, keepdims=True)
        return jnp.min(jnp.where(v == m, lane, big), axis=-1, keepdims=True)

    def pick(v, idx):
        return jnp.sum(jnp.where(lane == idx, v, 0.0), axis=-1, keepdims=True)

    is_g = lane < float(N_GROUPS)
    g_sel = first_argmax(jnp.where(is_g, biased, ninf))
    lg = jnp.where(is_g, logits, ninf)
    eg = jnp.exp(lg - jnp.max(lg, axis=-1, keepdims=True))
    g_w = pick(eg, g_sel) / jnp.sum(eg, axis=-1, keepdims=True)

    lo = float(EXPERT_LANE0) + float(EXP_PER_GROUP) * g_sel
    is_e = (lane >= lo) & (lane < lo + float(EXP_PER_GROUP))
    eb = jnp.where(is_e, biased, ninf)
    i1 = first_argmax(eb)
    i2 = first_argmax(jnp.where(lane == i1, ninf, eb))
    le = jnp.where(is_e, logits, ninf)
    ee = jnp.exp(le - jnp.max(le, axis=-1, keepdims=True))
    den = jnp.sum(ee, axis=-1, keepdims=True)
    p1 = pick(ee, i1) / den
    p2 = pick(ee, i2) / den
    w1 = g_w * (p1 / (p1 + p2))
    w2 = g_w * (p2 / (p1 + p2))

    hot1 = lane == i1
    hot2 = lane == i2
    cnt = jnp.where(hot1 | hot2, 1.0, 0.0)
    row = lax.broadcasted_iota(I32, (tm, tm), 0)
    col = lax.broadcasted_iota(I32, (tm, tm), 1)
    ltri = jnp.where(col < row, 1.0, 0.0).astype(BF16)
    rank = _dot(ltri, cnt.astype(BF16))
    r1 = jnp.sum(jnp.where(hot1, rank, 0.0), axis=-1, keepdims=True)
    r2 = jnp.sum(jnp.where(hot2, rank, 0.0), axis=-1, keepdims=True)

    n_row = jnp.sum(cnt, axis=0, keepdims=True)
    chunks = jnp.floor((n_row + float(SUBLANES - 1)) * (1.0 / SUBLANES))
    lrow = lax.broadcasted_iota(I32, (ROUTER_LANES, ROUTER_LANES), 0)
    lcol = lax.broadcasted_iota(I32, (ROUTER_LANES, ROUTER_LANES), 1)
    before = jnp.where(lrow < lcol, 1.0, 0.0).astype(BF16)
    chunks8 = jnp.broadcast_to(chunks, (SUBLANES, ROUTER_LANES))
    so_rows = _dot(chunks8.astype(BF16), before) * float(SUBLANES)
    srow = lax.broadcasted_iota(I32, (SUBLANES, ROUTER_LANES), 0)
    tab_ref[...] = jnp.where(srow == 0, jnp.broadcast_to(n_row, (SUBLANES, ROUTER_LANES)), so_rows)

    meta = jnp.where(lane == 0.0, w1, 0.0)
    for j, v in enumerate((w2, i1, i2, r1, r2), start=1):
        meta = jnp.where(lane == float(j), v, meta)
    meta_ref[...] = meta

    s1, s2 = _stage_selectors(meta, so_rows[0:1])
    sel = jnp.where(s1 | s2, 1.0, 0.0).astype(BF16)
    xs_ref[...] = _dot_tn(sel, xn.astype(BF16))


def _router(layer, h, g, wr, br):
    t, d = h.shape
    tm = TM_MOE
    nt = t // tm
    return pl.pallas_call(
        _router_kernel,
        grid=(nt,),
        in_specs=[pl.BlockSpec((tm, d), lambda i: (i, 0))] + [_layer_spec(a, layer) for a in (g, wr, br)],
        out_specs=[
            pl.BlockSpec((STAGE_ROWS, d), lambda i: (i, 0)),
            pl.BlockSpec((tm, ROUTER_LANES), lambda i: (i, 0)),
            pl.BlockSpec((SUBLANES, ROUTER_LANES), lambda i: (i, 0)),
        ],
        out_shape=[
            jax.ShapeDtypeStruct((nt * STAGE_ROWS, d), F32),
            jax.ShapeDtypeStruct((t, ROUTER_LANES), F32),
            jax.ShapeDtypeStruct((nt * SUBLANES, ROUTER_LANES), F32),
        ],
        compiler_params=pltpu.CompilerParams(dimension_semantics=("arbitrary",), vmem_limit_bytes=VMEM_LIMIT),
        name="router",
    )(h, g, wr, br)


def _chunk_table_kernel(nch_ref, src_ref, dst_ref, tab_ref):
    def fill(c, carry):
        tab_ref[c] = ZERO_CHUNK
        return carry

    lax.fori_loop(0, tab_ref.shape[0], fill, 0, unroll=8)

    def run(r, carry):
        src = src_ref[r]
        dst = dst_ref[r]

        def one(c, c2):
            tab_ref[dst + c] = src + c
            return c2

        lax.fori_loop(0, nch_ref[r], one, 0)
        return carry

    lax.fori_loop(0, nch_ref.shape[0], run, 0)


def _chunk_table(nch, src, dst, n_chunks):
    smem = pl.BlockSpec(memory_space=pltpu.SMEM)
    return pl.pallas_call(
        _chunk_table_kernel,
        in_specs=[smem] * 3,
        out_specs=smem,
        out_shape=jax.ShapeDtypeStruct((n_chunks,), I32),
        name="chunks",
    )(nch, src, dst)


def _expert_kernel(be_ref, nu_ref, tab_ref, xs_hbm, wg_ref, wu_ref, wd_ref, y_ref, xbuf, wgb, wub, wdb, sem):
    b = pl.program_id(0)
    nu = nu_ref[0]
    bm = xbuf.shape[1]
    per_block = bm // SUBLANES

    def start_gather(blk):
        slot = blk % 2
        for c in range(per_block):
            src = pl.multiple_of(tab_ref[blk * per_block + c] * SUBLANES, SUBLANES)
            pltpu.make_async_copy(xs_hbm.at[pl.ds(src, SUBLANES)],
                                  xbuf.at[slot, pl.ds(c * SUBLANES, SUBLANES)], sem.at[slot]).start()

    def wait_gather(blk):
        slot = blk % 2
        pltpu.make_async_copy(xs_hbm.at[pl.ds(0, bm)], xbuf.at[slot], sem.at[slot]).wait()

    @pl.when(b == 0)
    def _():
        start_gather(0)

    @pl.when(b + 1 < nu)
    def _():
        start_gather(b + 1)

    @pl.when(b < nu)
    def _():
        e = be_ref[b]
        changed = (b == 0) | (e != be_ref[jnp.maximum(b - 1, 0)])

        @pl.when(changed)
        def _():
            wgb[...] = wg_ref[...].astype(BF16)
            wub[...] = wu_ref[...].astype(BF16)
            wdb[...] = wd_ref[...].astype(BF16)

        wait_gather(b)
        x = xbuf[b % 2].astype(BF16)
        a = _dot(x, wgb[...])
        u = _dot(x, wub[...])
        hmid = (a * _sigmoid(a)) * u
        y_ref[...] = _dot(hmid.astype(BF16), wdb[...])

    @pl.when(b >= nu)
    def _():
        y_ref[...] = jnp.zeros_like(y_ref)


def _experts(layer, block_e, n_used, tab, xs, wg, wu, wd):
    d = xs.shape[1]
    de = wg.shape[-1]
    nb = block_e.shape[0]
    wmap = lambda b, be, nu, tb: (layer, be[b], 0, 0)
    return pl.pallas_call(
        _expert_kernel,
        grid_spec=pltpu.PrefetchScalarGridSpec(
            num_scalar_prefetch=3,
            grid=(nb,),
            in_specs=[
                pl.BlockSpec(memory_space=pl.ANY),
                pl.BlockSpec((None, None, d, de), wmap),
                pl.BlockSpec((None, None, d, de), wmap),
                pl.BlockSpec((None, None, de, d), wmap),
            ],
            out_specs=pl.BlockSpec((BM_EXP, d), lambda b, be, nu, tb: (b, 0)),
            scratch_shapes=[
                pltpu.VMEM((2, BM_EXP, d), F32),
                pltpu.VMEM((d, de), BF16), pltpu.VMEM((d, de), BF16), pltpu.VMEM((de, d), BF16),
                pltpu.SemaphoreType.DMA((2,)),
            ],
        ),
        out_shape=jax.ShapeDtypeStruct((nb * BM_EXP, d), F32),
        compiler_params=pltpu.CompilerParams(dimension_semantics=("arbitrary",), vmem_limit_bytes=VMEM_LIMIT),
        name="experts",
    )(block_e, n_used, tab, xs, wg, wu, wd)


def _combine_kernel(nch_ref, src_ref, dst_ref, h_ref, meta_ref, tab_ref, p_ref, wpe_ref, npe_ref, npg_ref,
                    wpg_ref, nf_ref, y_hbm, o_ref, stage, sem, *, final):
    i = pl.program_id(0)

    def chunk_copy(src_chunk, slot, dst_chunk):
        src = pl.multiple_of(src_chunk * SUBLANES, SUBLANES)
        dst = pl.multiple_of(dst_chunk * SUBLANES, SUBLANES)
        return pltpu.make_async_copy(y_hbm.at[pl.ds(src, SUBLANES)], stage.at[slot, pl.ds(dst, SUBLANES)],
                                     sem.at[slot])

    def for_each_chunk(tile, fn):
        def run(e, carry):
            r = tile * N_EXPERTS + e
            src = src_ref[r]
            dst = dst_ref[r]

            def one(c, c2):
                fn(src + c, dst + c)
                return c2

            lax.fori_loop(0, nch_ref[r], one, 0)
            return carry

        lax.fori_loop(0, N_EXPERTS, run, 0)

    def start_gather(tile):
        for_each_chunk(tile, lambda s, d: chunk_copy(s, tile % 2, d).start())

    @pl.when(i == 0)
    def _():
        stage[...] = jnp.zeros_like(stage)
        start_gather(0)

    @pl.when(i + 1 < pl.num_programs(0))
    def _():
        start_gather(i + 1)

    emb = _rms(_dot(p_ref[...].astype(BF16), wpe_ref[...]), npe_ref[...])

    slot = i % 2
    for_each_chunk(i, lambda s, d: chunk_copy(0, slot, 0).wait())

    meta = meta_ref[...]
    s1, s2 = _stage_selectors(meta, tab_ref[1:2, :])
    yb = stage[slot].astype(BF16)
    y1 = _dot(jnp.where(s1, 1.0, 0.0).astype(BF16), yb)
    y2 = _dot(jnp.where(s2, 1.0, 0.0).astype(BF16), yb)
    h2 = h_ref[...] + (y1 * meta[:, 0:1] + y2 * meta[:, 1:2])
    gate = _sigmoid(_dot(_rms(h2, npg_ref[...]).astype(BF16), wpg_ref[...]))
    out = h2 + gate * emb
    if final:
        out = _rms(out, nf_ref[...])
    o_ref[...] = out


def _combine(layer, nch, src, dst, h, meta, tab, p, wpe, npe, npg, wpg, nf, y, final):
    t, d = h.shape
    tm = TM_MOE
    tile = lambda r, w: pl.BlockSpec((r, w), lambda i, *s: (i, 0))
    return pl.pallas_call(
        functools.partial(_combine_kernel, final=final),
        grid_spec=pltpu.PrefetchScalarGridSpec(
            num_scalar_prefetch=3,
            grid=(t // tm,),
            in_specs=[
                tile(tm, d), tile(tm, ROUTER_LANES), tile(SUBLANES, ROUTER_LANES),
                pl.BlockSpec((None, None, tm, p.shape[-1]), lambda i, *s: (layer, 0, i, 0)),
                _layer_spec(wpe, layer), _layer_spec(npe, layer), _layer_spec(npg, layer), _layer_spec(wpg, layer),
                pl.BlockSpec(nf.shape, lambda i, *s: (0, 0)),
                pl.BlockSpec(memory_space=pl.ANY),
            ],
            out_specs=tile(tm, d),
            scratch_shapes=[pltpu.VMEM((2, STAGE_ROWS, d), F32), pltpu.SemaphoreType.DMA((2,))],
        ),
        out_shape=jax.ShapeDtypeStruct((t, d), F32),
        compiler_params=pltpu.CompilerParams(dimension_semantics=("arbitrary",), vmem_limit_bytes=VMEM_LIMIT),
        name="combine",
    )(nch, src, dst, h, meta, tab, p, wpe, npe, npg, wpg, nf, y)


def _routing_tables(tab, t):
    nt = t // TM_MOE
    tab = tab.reshape(nt, SUBLANES, ROUTER_LANES)[:, :, EXPERT_LANE0:EXPERT_LANE0 + N_EXPERTS]
    counts = tab[:, 0].astype(I32)
    stage_chunk = tab[:, 1].astype(I32) // SUBLANES
    nch = (counts + SUBLANES - 1) // SUBLANES
    per_block = BM_EXP // SUBLANES
    seg = jnp.sum(nch, axis=0)
    padded = (seg + per_block - 1) // per_block * per_block
    pend = jnp.cumsum(padded)
    pstart = pend - padded
    run_chunk = pstart[None, :] + jnp.cumsum(nch, axis=0) - nch
    nb = -(-(TOP_K * t + (SUBLANES - 1) * nt * N_EXPERTS) // BM_EXP) + N_EXPERTS
    bstart = jnp.arange(nb, dtype=I32) * per_block
    block_e = jnp.minimum(jnp.sum((pend[None, :] <= bstart[:, None]).astype(I32), axis=1), N_EXPERTS - 1)
    n_used = pend[-1:] // per_block
    tile_chunk0 = jnp.arange(nt, dtype=I32)[:, None] * (STAGE_ROWS // SUBLANES)
    flat = lambda a: a.reshape(-1).astype(I32)
    return (flat(nch), flat(stage_chunk), flat(tile_chunk0 + stage_chunk), flat(run_chunk),
            block_e.astype(I32), n_used.astype(I32), nb)


def kernel(x, p, norm_mix, w_in, lb_logits, rec_out_gain, sgu_ln_g, sgu_ln_b, sgu_w, sgu_b, w_out, norm_ffn, w_rg, b_rg, w_re, b_re, w_gate, w_up, w_down, norm_pg, w_pg, w_pe, norm_pe, norm_f):
    bsz, seq, d = x.shape
    depth = w_in.shape[0]
    assert bsz == 1 and seq % TM_MOE == 0 and seq % TM_MIX == 0
    h = x.reshape(seq, d)

    lb = jnp.cumsum(jax.nn.softmax(lb_logits.astype(F32), axis=0), axis=0)
    lb = lb - lb[0]
    rows = lambda v: v.reshape(depth, 1, -1)
    la, l1, oml = rows(jnp.log(lb)), rows(jnp.log1p(-lb)), rows(1.0 - lb)
    gain = rows(jnp.tile(rec_out_gain, (1, H_REC)))
    sb = jnp.repeat(jnp.swapaxes(sgu_b, 1, 2), D_SGU // H_SGU, axis=2)
    win_b, wout_b, wpg_b, wpe_b = (w.astype(BF16) for w in (w_in, w_out, w_pg, w_pe))
    npad = ROUTER_LANES - N_GROUPS - N_EXPERTS
    wr = jnp.concatenate([w_rg, w_re, jnp.zeros((depth, d, npad), F32)], axis=2)
    br = rows(jnp.concatenate([b_rg, b_re, jnp.zeros((depth, npad), F32)], axis=1))
    wsel = jnp.asarray(_decay_selectors(), BF16)
    lev = jnp.asarray(_pair_levels())
    gmix, gffn, gpg, gpe, lng, lnb = (rows(v) for v in (norm_mix, norm_ffn, norm_pg, norm_pe, sgu_ln_g, sgu_ln_b))
    nf = norm_f.reshape(1, d)

    for l in range(depth):
        h = _mixer(l, h, gmix, win_b, la, l1, oml, gain, lng, lnb, sgu_w, sb, wout_b, wsel, lev)
        xs, meta, tab = _router(l, h, gffn, wr, br)
        nch, stage_chunk, xs_chunk, run_chunk, block_e, n_used, nb = _routing_tables(tab, seq)
        ctab = _chunk_table(nch, xs_chunk, run_chunk, nb * (BM_EXP // SUBLANES))
        y = _experts(l, block_e, n_used, ctab, xs, w_gate, w_up, w_down)
        h = _combine(l, nch, run_chunk, stage_chunk, h, meta, tab, p, wpe_b, gpe, gpg, wpg_b, nf, y,
                     final=(l == depth - 1))
    return h.reshape(bsz, seq, d)
```

```python
import functools

import numpy as np
import jax
import jax.numpy as jnp
from jax import lax
from jax.experimental import pallas as pl
from jax.experimental.pallas import tpu as pltpu

F32 = jnp.float32
BF16 = jnp.bfloat16
I32 = jnp.int32

EPS = 1e-6
D_REC = 512
D_SGU = 512
H_REC = 4
DK_REC = 128
CHUNK_REC = 64
SGU_CHUNK = 128
H_SGU = 4
N_GROUPS = 4
EXP_PER_GROUP = 8
N_EXPERTS = N_GROUPS * EXP_PER_GROUP
TOP_K = 2
ROUTER_LANES = 128
EXPERT_LANE0 = N_GROUPS
SUBLANES = 8

TM_MIX = 512
TM_MOE = 256
STAGE_ROWS = 768
BM_EXP = 512
VMEM_LIMIT = 48 * 1024 * 1024

assert STAGE_ROWS >= TOP_K * TM_MOE + (SUBLANES - 1) * N_EXPERTS + SUBLANES
ZERO_CHUNK = STAGE_ROWS // SUBLANES - 1

_LEVELS = (2, 4, 8, 16, 32, 64)


def _decay_selectors():
    t = np.arange(CHUNK_REC)[:, None]
    s = np.arange(CHUNK_REC)[None, :]
    mats = [((t // m == s // m) & (s <= t)) for m in _LEVELS]
    mats += [((t // m == s // m) & (s > t)) for m in _LEVELS]
    return np.concatenate(mats, axis=0).astype(np.float32)


def _pair_levels():
    t = np.arange(CHUNK_REC)[:, None]
    s = np.arange(CHUNK_REC)[None, :]
    x = t ^ s
    lev = np.floor(np.log2(np.maximum(x, 1))).astype(np.int32)
    lev = np.where(t == s, -1, lev)
    lev = np.where(s > t, -2, lev)
    return lev.astype(np.int32)


def _rms(x, g):
    return x * lax.rsqrt(jnp.mean(x * x, axis=-1, keepdims=True) + EPS) * g


def _sigmoid(x):
    return 1.0 / (1.0 + jnp.exp(-x))


def _gelu_tanh(x):
    return x * (0.5 * (1.0 + jnp.tanh(0.7978845608028654 * (x + 0.044715 * (x * x * x)))))


def _dot(a, b):
    return jnp.dot(a, b, preferred_element_type=F32)


def _dot_nt(a, b):
    return lax.dot_general(a, b, (((1,), (1,)), ((), ())), preferred_element_type=F32)


def _dot_tn(a, b):
    return lax.dot_general(a, b, (((0,), (0,)), ((), ())), preferred_element_type=F32)


def _layer_spec(arr, layer):
    zeros = (0,) * (arr.ndim - 1)
    return pl.BlockSpec((None,) + arr.shape[1:], lambda *a: (layer,) + zeros)


def _mixer_kernel(h_ref, gmix_ref, win_ref, la_ref, l1_ref, oml_ref, gain_ref, lng_ref, lnb_ref,
                  ws_ref, sb_ref, wout_ref, wsel_ref, lev_ref, o_ref, st_ref, ocat_ref):
    tm = h_ref.shape[0]

    @pl.when(pl.program_id(0) == 0)
    def _():
        st_ref[...] = jnp.zeros_like(st_ref)

    h = h_ref[...]
    hb = _rms(h, gmix_ref[...]).astype(BF16)

    def proj(j):
        return _dot(hb, win_ref[:, j * 512:(j + 1) * 512])

    zq = proj(0)
    q = zq * _sigmoid(zq)
    fz = proj(1)
    iv = proj(2).astype(BF16)
    zg = proj(3)
    gate = zg * _sigmoid(zg)

    ls = jnp.minimum(fz, 0.0) - jnp.log1p(jnp.exp(-jnp.abs(fz)))
    c = l1_ref[...] + ls
    la = la_ref[...]
    lf = jnp.maximum(la, c) + jnp.log1p(jnp.exp(-jnp.abs(la - c)))
    kk = oml_ref[...] * (1.0 / (1.0 + jnp.exp(fz)))

    wsel = wsel_ref[...]
    lev = lev_ref[...]
    gain = gain_ref[...]
    nlev = len(_LEVELS)

    for cidx in range(tm // CHUNK_REC):
        sl = slice(cidx * CHUNK_REC, (cidx + 1) * CHUNK_REC)
        lf_c = lf[sl]
        hi = lf_c.astype(BF16)
        rem = lf_c - hi.astype(F32)
        mid = rem.astype(BF16)
        lo = (rem - mid.astype(F32)).astype(BF16)
        pw = jnp.exp(_dot(wsel, jnp.concatenate([hi, mid, lo], axis=0)))
        f_c = jnp.exp(lf_c)
        for hh in range(H_REC):
            hs = slice(hh * DK_REC, (hh + 1) * DK_REC)
            q_h = q[sl, hs]
            k_h = kk[sl, hs]
            i_h = iv[sl, hs]
            k_b = k_h.astype(BF16)
            sc = jnp.where(lev == -1, _dot_nt(q_h.astype(BF16), k_b), 0.0)
            sc = sc + jnp.where(lev == 0, _dot_nt((q_h * f_c[:, hs]).astype(BF16), k_b), 0.0)
            for j in range(nlev - 1):
                qd = (q_h * pw[j * 64:(j + 1) * 64, hs]).astype(BF16)
                kd = (k_h * pw[(nlev + j) * 64:(nlev + j + 1) * 64, hs]).astype(BF16)
                sc = sc + jnp.where(lev == j + 1, _dot_nt(qd, kd), 0.0)
            jl = nlev - 1
            qb = (q_h * pw[jl * 64:(jl + 1) * 64, hs]).astype(BF16)
            kd = (k_h * pw[(nlev + jl) * 64:(nlev + jl + 1) * 64, hs]).astype(BF16)
            st = st_ref[hh]
            o = _dot(sc.astype(BF16), i_h) + _dot_nt(qb, st.astype(BF16))
            dec = pw[jl * 64 + 63:jl * 64 + 64, hs]
            st_ref[hh] = st * dec + _dot_tn(i_h, kd)
            o = o * lax.rsqrt(jnp.mean(o * o, axis=-1, keepdims=True) + EPS) * gain[:, hs]
            ocat_ref[sl, hs] = (o * gate[sl, hs]).astype(BF16)

    zu = proj(4)
    zv = proj(5)
    r128 = lax.broadcasted_iota(I32, (SGU_CHUNK, SGU_CHUNK), 0)
    c128 = lax.broadcasted_iota(I32, (SGU_CHUNK, SGU_CHUNK), 1)
    wtri = [jnp.where(c128 <= r128, ws_ref[hh], 0.0).astype(BF16) for hh in range(H_SGU)]
    for cidx in range(tm // SGU_CHUNK):
        sl = slice(cidx * SGU_CHUNK, (cidx + 1) * SGU_CHUNK)
        ua = _gelu_tanh(zu[sl])
        va = _gelu_tanh(zv[sl])
        mu = jnp.mean(va, axis=-1, keepdims=True)
        vc = va - mu
        var = jnp.mean(vc * vc, axis=-1, keepdims=True)
        vn = (vc * lax.rsqrt(var + EPS) * lng_ref[...] + lnb_ref[...]).astype(BF16)
        for hh in range(H_SGU):
            hs = slice(hh * 128, (hh + 1) * 128)
            s = _dot(wtri[hh], vn[:, hs]) + sb_ref[:, hs]
            ocat_ref[sl, D_REC + hh * 128:D_REC + (hh + 1) * 128] = (ua[:, hs] * s).astype(BF16)

    o_ref[...] = h + _dot(ocat_ref[...], wout_ref[...])


def _mixer(layer, h, gmix, win, la, l1, oml, gain, lng, lnb, ws, sb, wout, wsel, lev):
    t, d = h.shape
    tm = min(TM_MIX, t)
    per_layer = (gmix, win, la, l1, oml, gain, lng, lnb, ws, sb, wout)
    return pl.pallas_call(
        _mixer_kernel,
        grid=(t // tm,),
        in_specs=[pl.BlockSpec((tm, d), lambda i: (i, 0))]
        + [_layer_spec(a, layer) for a in per_layer]
        + [pl.BlockSpec(wsel.shape, lambda i: (0, 0)), pl.BlockSpec(lev.shape, lambda i: (0, 0))],
        out_specs=pl.BlockSpec((tm, d), lambda i: (i, 0)),
        out_shape=jax.ShapeDtypeStruct((t, d), F32),
        scratch_shapes=[pltpu.VMEM((H_REC, DK_REC, DK_REC), F32), pltpu.VMEM((tm, d), BF16)],
        compiler_params=pltpu.CompilerParams(dimension_semantics=("arbitrary",), vmem_limit_bytes=VMEM_LIMIT),
        name="mixer",
    )(h, *per_layer, wsel, lev)


def _stage_selectors(meta, so_row):
    tm = meta.shape[0]
    lane = lax.broadcasted_iota(I32, (tm, ROUTER_LANES), 1).astype(F32)
    col = lax.broadcasted_iota(I32, (tm, STAGE_ROWS), 1)
    sels = []
    for k in range(TOP_K):
        run0 = jnp.sum(jnp.where(lane == meta[:, 2 + k:3 + k], so_row, 0.0), axis=-1, keepdims=True)
        pos = (run0 + meta[:, 4 + k:5 + k]).astype(I32)
        sels.append(col == pos)
    return sels


def _router_kernel(h_ref, g_ref, wr_ref, br_ref, xs_ref, meta_ref, tab_ref):
    tm = h_ref.shape[0]
    xn = _rms(h_ref[...], g_ref[...])
    xh = xn.astype(BF16)
    xl = (xn - xh.astype(F32)).astype(BF16)
    part = _dot(xh, wr_ref[...])
    logits = part[:, :ROUTER_LANES] + part[:, ROUTER_LANES:] + _dot(xl, wr_ref[:, :ROUTER_LANES])
    biased = logits + br_ref[...]
    lane = lax.broadcasted_iota(I32, (tm, ROUTER_LANES), 1).astype(F32)
    ninf = -jnp.inf
    big = float(ROUTER_LANES)

    def first_argmax(v):
        m = jnp.max(v, axis=-1, keepdims=True)
        return jnp.min(jnp.where(v == m, lane, big), axis=-1, keepdims=True)

    def pick(v, idx):
        return jnp.sum(jnp.where(lane == idx, v, 0.0), axis=-1, keepdims=True)

    is_g = lane < float(N_GROUPS)
    g_sel = first_argmax(jnp.where(is_g, biased, ninf))
    lg = jnp.where(is_g, logits, ninf)
    eg = jnp.exp(lg - jnp.max(lg, axis=-1, keepdims=True))
    g_w = pick(eg, g_sel) / jnp.sum(eg, axis=-1, keepdims=True)

    lo = float(EXPERT_LANE0) + float(EXP_PER_GROUP) * g_sel
    is_e = (lane >= lo) & (lane < lo + float(EXP_PER_GROUP))
    eb = jnp.where(is_e, biased, ninf)
    i1 = first_argmax(eb)
    i2 = first_argmax(jnp.where(lane == i1, ninf, eb))
    le = jnp.where(is_e, logits, ninf)
    ee = jnp.exp(le - jnp.max(le, axis=-1, keepdims=True))
    den = jnp.sum(ee, axis=-1, keepdims=True)
    p1 = pick(ee, i1) / den
    p2 = pick(ee, i2) / den
    w1 = g_w * (p1 / (p1 + p2))
    w2 = g_w * (p2 / (p1 + p2))

    hot1 = lane == i1
    hot2 = lane == i2
    cnt = jnp.where(hot1 | hot2, 1.0, 0.0)
    row = lax.broadcasted_iota(I32, (tm, tm), 0)
    col = lax.broadcasted_iota(I32, (tm, tm), 1)
    ltri = jnp.where(col < row, 1.0, 0.0).astype(BF16)
    rank = _dot(ltri, cnt.astype(BF16))
    r1 = jnp.sum(jnp.where(hot1, rank, 0.0), axis=-1, keepdims=True)
    r2 = jnp.sum(jnp.where(hot2, rank, 0.0), axis=-1, keepdims=True)

    n_row = jnp.sum(cnt, axis=0, keepdims=True)
    chunks = jnp.floor((n_row + float(SUBLANES - 1)) * (1.0 / SUBLANES))
    lrow = lax.broadcasted_iota(I32, (ROUTER_LANES, ROUTER_LANES), 0)
    lcol = lax.broadcasted_iota(I32, (ROUTER_LANES, ROUTER_LANES), 1)
    before = jnp.where(lrow < lcol, 1.0, 0.0).astype(BF16)
    chunks8 = jnp.broadcast_to(chunks, (SUBLANES, ROUTER_LANES))
    so_rows = _dot(chunks8.astype(BF16), before) * float(SUBLANES)
    srow = lax.broadcasted_iota(I32, (SUBLANES, ROUTER_LANES), 0)
    tab_ref[...] = jnp.where(srow == 0, jnp.broadcast_to(n_row, (SUBLANES, ROUTER_LANES)), so_rows)

    meta = jnp.where(lane == 0.0, w1, 0.0)
    for j, v in enumerate((w2, i1, i2, r1, r2), start=1):
        meta = jnp.where(lane == float(j), v, meta)
    meta_ref[...] = meta

    s1, s2 = _stage_selectors(meta, so_rows[0:1])
    sel = jnp.where(s1 | s2, 1.0, 0.0).astype(BF16)
    xs_ref[...] = _dot_tn(sel, xh)


def _router(layer, h, g, wr, br):
    t, d = h.shape
    tm = TM_MOE
    nt = t // tm
    return pl.pallas_call(
        _router_kernel,
        grid=(nt,),
        in_specs=[pl.BlockSpec((tm, d), lambda i: (i, 0))] + [_layer_spec(a, layer) for a in (g, wr, br)],
        out_specs=[
            pl.BlockSpec((STAGE_ROWS, d), lambda i: (i, 0)),
            pl.BlockSpec((tm, ROUTER_LANES), lambda i: (i, 0)),
            pl.BlockSpec((SUBLANES, ROUTER_LANES), lambda i: (i, 0)),
        ],
        out_shape=[
            jax.ShapeDtypeStruct((nt * STAGE_ROWS, d), F32),
            jax.ShapeDtypeStruct((t, ROUTER_LANES), F32),
            jax.ShapeDtypeStruct((nt * SUBLANES, ROUTER_LANES), F32),
        ],
        compiler_params=pltpu.CompilerParams(dimension_semantics=("arbitrary",), vmem_limit_bytes=VMEM_LIMIT),
        name="router",
    )(h, g, wr, br)


def _chunk_table_kernel(nch_ref, src_ref, dst_ref, tab_ref):
    def fill(c, carry):
        tab_ref[c] = ZERO_CHUNK
        return carry

    lax.fori_loop(0, tab_ref.shape[0], fill, 0, unroll=8)

    def run(r, carry):
        src = src_ref[r]
        dst = dst_ref[r]

        def one(c, c2):
            tab_ref[dst + c] = src + c
            return c2

        lax.fori_loop(0, nch_ref[r], one, 0)
        return carry

    lax.fori_loop(0, nch_ref.shape[0], run, 0)


def _chunk_table(nch, src, dst, n_chunks):
    smem = pl.BlockSpec(memory_space=pltpu.SMEM)
    return pl.pallas_call(
        _chunk_table_kernel,
        in_specs=[smem] * 3,
        out_specs=smem,
        out_shape=jax.ShapeDtypeStruct((n_chunks,), I32),
        name="chunks",
    )(nch, src, dst)


def _expert_kernel(be_ref, nu_ref, tab_ref, xs_hbm, wg_ref, wu_ref, wd_ref, y_ref, xbuf, wgb, wub, wdb, sem):
    b = pl.program_id(0)
    nu = nu_ref[0]
    bm = xbuf.shape[1]
    per_block = bm // SUBLANES

    def start_gather(blk):
        slot = blk % 2
        for c in range(per_block):
            src = pl.multiple_of(tab_ref[blk * per_block + c] * SUBLANES, SUBLANES)
            pltpu.make_async_copy(xs_hbm.at[pl.ds(src, SUBLANES)],
                                  xbuf.at[slot, pl.ds(c * SUBLANES, SUBLANES)], sem.at[slot]).start()

    def wait_gather(blk):
        slot = blk % 2
        pltpu.make_async_copy(xs_hbm.at[pl.ds(0, bm)], xbuf.at[slot], sem.at[slot]).wait()

    @pl.when(b == 0)
    def _():
        start_gather(0)

    @pl.when(b + 1 < nu)
    def _():
        start_gather(b + 1)

    @pl.when(b < nu)
    def _():
        e = be_ref[b]
        changed = (b == 0) | (e != be_ref[jnp.maximum(b - 1, 0)])

        @pl.when(changed)
        def _():
            wgb[...] = wg_ref[...].astype(BF16)
            wub[...] = wu_ref[...].astype(BF16)
            wdb[...] = wd_ref[...].astype(BF16)

        wait_gather(b)
        x = xbuf[b % 2].astype(BF16)
        a = _dot(x, wgb[...])
        u = _dot(x, wub[...])
        hmid = (a * _sigmoid(a)) * u
        y_ref[...] = _dot(hmid.astype(BF16), wdb[...])

    @pl.when(b >= nu)
    def _():
        y_ref[...] = jnp.zeros_like(y_ref)


def _experts(layer, block_e, n_used, tab, xs, wg, wu, wd):
    d = xs.shape[1]
    de = wg.shape[-1]
    nb = block_e.shape[0]
    wmap = lambda b, be, nu, tb: (layer, be[b], 0, 0)
    return pl.pallas_call(
        _expert_kernel,
        grid_spec=pltpu.PrefetchScalarGridSpec(
            num_scalar_prefetch=3,
            grid=(nb,),
            in_specs=[
                pl.BlockSpec(memory_space=pl.ANY),
                pl.BlockSpec((None, None, d, de), wmap),
                pl.BlockSpec((None, None, d, de), wmap),
                pl.BlockSpec((None, None, de, d), wmap),
            ],
            out_specs=pl.BlockSpec((BM_EXP, d), lambda b, be, nu, tb: (b, 0)),
            scratch_shapes=[
                pltpu.VMEM((2, BM_EXP, d), F32),
                pltpu.VMEM((d, de), BF16), pltpu.VMEM((d, de), BF16), pltpu.VMEM((de, d), BF16),
                pltpu.SemaphoreType.DMA((2,)),
            ],
        ),
        out_shape=jax.ShapeDtypeStruct((nb * BM_EXP, d), F32),
        compiler_params=pltpu.CompilerParams(dimension_semantics=("arbitrary",), vmem_limit_bytes=VMEM_LIMIT),
        name="experts",
    )(block_e, n_used, tab, xs, wg, wu, wd)


def _combine_kernel(nch_ref, src_ref, dst_ref, h_ref, meta_ref, tab_ref, p_ref, wpe_ref, npe_ref, npg_ref,
                    wpg_ref, nf_ref, y_hbm, o_ref, stage, sem, *, final):
    i = pl.program_id(0)

    def chunk_copy(src_chunk, slot, dst_chunk):
        src = pl.multiple_of(src_chunk * SUBLANES, SUBLANES)
        dst = pl.multiple_of(dst_chunk * SUBLANES, SUBLANES)
        return pltpu.make_async_copy(y_hbm.at[pl.ds(src, SUBLANES)], stage.at[slot, pl.ds(dst, SUBLANES)],
                                     sem.at[slot])

    def for_each_chunk(tile, fn):
        def run(e, carry):
            r = tile * N_EXPERTS + e
            src = src_ref[r]
            dst = dst_ref[r]

            def one(c, c2):
                fn(src + c, dst + c)
                return c2

            lax.fori_loop(0, nch_ref[r], one, 0)
            return carry

        lax.fori_loop(0, N_EXPERTS, run, 0)

    def start_gather(tile):
        for_each_chunk(tile, lambda s, d: chunk_copy(s, tile % 2, d).start())

    @pl.when(i == 0)
    def _():
        stage[...] = jnp.zeros_like(stage)
        start_gather(0)

    @pl.when(i + 1 < pl.num_programs(0))
    def _():
        start_gather(i + 1)

    emb = _rms(_dot(p_ref[...].astype(BF16), wpe_ref[...]), npe_ref[...])

    slot = i % 2
    for_each_chunk(i, lambda s, d: chunk_copy(0, slot, 0).wait())

    meta = meta_ref[...]
    s1, s2 = _stage_selectors(meta, tab_ref[1:2, :])
    sel = (jnp.where(s1, meta[:, 0:1], 0.0) + jnp.where(s2, meta[:, 1:2], 0.0)).astype(BF16)
    h2 = h_ref[...] + _dot(sel, stage[slot].astype(BF16))
    gate = _sigmoid(_dot(_rms(h2, npg_ref[...]).astype(BF16), wpg_ref[...]))
    out = h2 + gate * emb
    if final:
        out = _rms(out, nf_ref[...])
    o_ref[...] = out


def _combine(layer, nch, src, dst, h, meta, tab, p, wpe, npe, npg, wpg, nf, y, final):
    t, d = h.shape
    tm = TM_MOE
    tile = lambda r, w: pl.BlockSpec((r, w), lambda i, *s: (i, 0))
    return pl.pallas_call(
        functools.partial(_combine_kernel, final=final),
        grid_spec=pltpu.PrefetchScalarGridSpec(
            num_scalar_prefetch=3,
            grid=(t // tm,),
            in_specs=[
                tile(tm, d), tile(tm, ROUTER_LANES), tile(SUBLANES, ROUTER_LANES),
                pl.BlockSpec((None, None, tm, p.shape[-1]), lambda i, *s: (layer, 0, i, 0)),
                _layer_spec(wpe, layer), _layer_spec(npe, layer), _layer_spec(npg, layer), _layer_spec(wpg, layer),
                pl.BlockSpec(nf.shape, lambda i, *s: (0, 0)),
                pl.BlockSpec(memory_space=pl.ANY),
            ],
            out_specs=tile(tm, d),
            scratch_shapes=[pltpu.VMEM((2, STAGE_ROWS, d), F32), pltpu.SemaphoreType.DMA((2,))],
        ),
        out_shape=jax.ShapeDtypeStruct((t, d), F32),
        compiler_params=pltpu.CompilerParams(dimension_semantics=("arbitrary",), vmem_limit_bytes=VMEM_LIMIT),
        name="combine",
    )(nch, src, dst, h, meta, tab, p, wpe, npe, npg, wpg, nf, y)


def _routing_tables(tab, t):
    nt = t // TM_MOE
    tab = tab.reshape(nt, SUBLANES, ROUTER_LANES)[:, :, EXPERT_LANE0:EXPERT_LANE0 + N_EXPERTS]
    counts = tab[:, 0].astype(I32)
    stage_chunk = tab[:, 1].astype(I32) // SUBLANES
    nch = (counts + SUBLANES - 1) // SUBLANES
    per_block = BM_EXP // SUBLANES
    seg = jnp.sum(nch, axis=0)
    padded = (seg + per_block - 1) // per_block * per_block
    pend = jnp.cumsum(padded)
    pstart = pend - padded
    run_chunk = pstart[None, :] + jnp.cumsum(nch, axis=0) - nch
    nb = -(-(TOP_K * t + (SUBLANES - 1) * nt * N_EXPERTS) // BM_EXP) + N_EXPERTS
    bstart = jnp.arange(nb, dtype=I32) * per_block
    block_e = jnp.minimum(jnp.sum((pend[None, :] <= bstart[:, None]).astype(I32), axis=1), N_EXPERTS - 1)
    n_used = pend[-1:] // per_block
    tile_chunk0 = jnp.arange(nt, dtype=I32)[:, None] * (STAGE_ROWS // SUBLANES)
    flat = lambda a: a.reshape(-1).astype(I32)
    return (flat(nch), flat(stage_chunk), flat(tile_chunk0 + stage_chunk), flat(run_chunk),
            block_e.astype(I32), n_used.astype(I32), nb)


def kernel(x, p, norm_mix, w_in, lb_logits, rec_out_gain, sgu_ln_g, sgu_ln_b, sgu_w, sgu_b, w_out, norm_ffn, w_rg, b_rg, w_re, b_re, w_gate, w_up, w_down, norm_pg, w_pg, w_pe, norm_pe, norm_f):
    bsz, seq, d = x.shape
    depth = w_in.shape[0]
    assert bsz == 1 and seq % TM_MOE == 0 and seq % TM_MIX == 0
    h = x.reshape(seq, d)

    lb = jnp.cumsum(jax.nn.softmax(lb_logits.astype(F32), axis=0), axis=0)
    lb = lb - lb[0]
    rows = lambda v: v.reshape(depth, 1, -1)
    la, l1, oml = rows(jnp.log(lb)), rows(jnp.log1p(-lb)), rows(1.0 - lb)
    gain = rows(jnp.tile(rec_out_gain, (1, H_REC)))
    sb = jnp.repeat(jnp.swapaxes(sgu_b, 1, 2), D_SGU // H_SGU, axis=2)
    win_b, wout_b, wpg_b, wpe_b = (w.astype(BF16) for w in (w_in, w_out, w_pg, w_pe))
    npad = ROUTER_LANES - N_GROUPS - N_EXPERTS
    wr = jnp.concatenate([w_rg, w_re, jnp.zeros((depth, d, npad), F32)], axis=2)
    wr_hi = wr.astype(BF16)
    wr = jnp.concatenate([wr_hi, (wr - wr_hi.astype(F32)).astype(BF16)], axis=2)
    br = rows(jnp.concatenate([b_rg, b_re, jnp.zeros((depth, npad), F32)], axis=1))
    wsel = jnp.asarray(np.tile(_decay_selectors(), (1, 3)), BF16)
    lev = jnp.asarray(_pair_levels())
    gmix, gffn, gpg, gpe, lng, lnb = (rows(v) for v in (norm_mix, norm_ffn, norm_pg, norm_pe, sgu_ln_g, sgu_ln_b))
    nf = norm_f.reshape(1, d)

    for l in range(depth):
        h = _mixer(l, h, gmix, win_b, la, l1, oml, gain, lng, lnb, sgu_w, sb, wout_b, wsel, lev)
        xs, meta, tab = _router(l, h, gffn, wr, br)
        nch, stage_chunk, xs_chunk, run_chunk, block_e, n_used, nb = _routing_tables(tab, seq)
        ctab = _chunk_table(nch, xs_chunk, run_chunk, nb * (BM_EXP // SUBLANES))
        y = _experts(l, block_e, n_used, ctab, xs, w_gate, w_up, w_down)
        h = _combine(l, nch, run_chunk, stage_chunk, h, meta, tab, p, wpe_b, gpe, gpg, wpg_b, nf, y,
                     final=(l == depth - 1))
    return h.reshape(bsz, seq, d)
```

```python
import functools

import numpy as np
import jax
import jax.numpy as jnp
from jax import lax
from jax.experimental import pallas as pl
from jax.experimental.pallas import tpu as pltpu

F32 = jnp.float32
BF16 = jnp.bfloat16
I32 = jnp.int32

EPS = 1e-6
D_REC = 512
D_SGU = 512
H_REC = 4
DK_REC = 128
CHUNK_REC = 64
SGU_CHUNK = 128
H_SGU = 4
N_GROUPS = 4
EXP_PER_GROUP = 8
N_EXPERTS = N_GROUPS * EXP_PER_GROUP
TOP_K = 2
ROUTER_LANES = 128
EXPERT_LANE0 = N_GROUPS
SUBLANES = 8

TM_MIX = 512
TM_MOE = 256
STAGE_ROWS = 768
BM_EXP = 512
VMEM_LIMIT = 48 * 1024 * 1024

assert STAGE_ROWS >= TOP_K * TM_MOE + (SUBLANES - 1) * N_EXPERTS + SUBLANES
STAGE_CHUNKS = STAGE_ROWS // SUBLANES
ZERO_CHUNK = STAGE_CHUNKS - 1
HIGH_HALF = -65536

_LEVELS = (2, 4, 8, 16, 32, 64)


def _decay_selectors():
    t = np.arange(CHUNK_REC)[:, None]
    s = np.arange(CHUNK_REC)[None, :]
    mats = [((t // m == s // m) & (s <= t)) for m in _LEVELS]
    mats += [((t // m == s // m) & (s > t)) for m in _LEVELS]
    return np.concatenate(mats, axis=0).astype(np.float32)


def _pair_levels():
    t = np.arange(CHUNK_REC)[:, None]
    s = np.arange(CHUNK_REC)[None, :]
    x = t ^ s
    lev = np.floor(np.log2(np.maximum(x, 1))).astype(np.int32)
    lev = np.where(t == s, -1, lev)
    lev = np.where(s > t, -2, lev)
    return lev.astype(np.int32)


def _rms(x, g):
    return x * lax.rsqrt(jnp.mean(x * x, axis=-1, keepdims=True) + EPS) * g


def _sigmoid(x):
    return 1.0 / (1.0 + jnp.exp(-x))


def _gelu_tanh(x):
    return x * (0.5 * (1.0 + jnp.tanh(0.7978845608028654 * (x + 0.044715 * (x * x * x)))))


def _dot(a, b):
    return jnp.dot(a, b, preferred_element_type=F32)


def _dot_nt(a, b):
    return lax.dot_general(a, b, (((1,), (1,)), ((), ())), preferred_element_type=F32)


def _dot_tn(a, b):
    return lax.dot_general(a, b, (((0,), (0,)), ((), ())), preferred_element_type=F32)


def _pack_pairs(x):
    bits = lax.bitcast_convert_type(x, I32)
    half = x.shape[1] // 2
    return lax.shift_right_logical(bits[:, :half], 16) | (bits[:, half:] & HIGH_HALF)


def _unpack_pairs(w):
    lo = lax.bitcast_convert_type(lax.shift_left(w, 16), F32)
    hi = lax.bitcast_convert_type(w & HIGH_HALF, F32)
    return jnp.concatenate([lo, hi], axis=1).astype(BF16)


def _layer_spec(arr, layer):
    zeros = (0,) * (arr.ndim - 1)
    return pl.BlockSpec((None,) + arr.shape[1:], lambda *a: (layer,) + zeros)


def _mixer_kernel(h_ref, gmix_ref, win_ref, la_ref, l1_ref, oml_ref, gain_ref, lng_ref, lnb_ref,
                  ws_ref, sb_ref, wout_ref, wsel_ref, lev_ref, o_ref, st_ref, ocat_ref):
    tm = h_ref.shape[0]

    @pl.when(pl.program_id(0) == 0)
    def _():
        st_ref[...] = jnp.zeros_like(st_ref)

    h = h_ref[...]
    hb = _rms(h, gmix_ref[...]).astype(BF16)

    def proj(j):
        return _dot(hb, win_ref[:, j * 512:(j + 1) * 512])

    zq = proj(0)
    q = zq * _sigmoid(zq)
    fz = proj(1)
    iv = proj(2).astype(BF16)
    zg = proj(3)
    gate = zg * _sigmoid(zg)

    ls = jnp.minimum(fz, 0.0) - jnp.log1p(jnp.exp(-jnp.abs(fz)))
    c = l1_ref[...] + ls
    la = la_ref[...]
    lf = jnp.maximum(la, c) + jnp.log1p(jnp.exp(-jnp.abs(la - c)))
    kk = oml_ref[...] * (1.0 / (1.0 + jnp.exp(fz)))

    wsel = wsel_ref[...]
    lev = lev_ref[...]
    gain = gain_ref[...]
    nlev = len(_LEVELS)

    for cidx in range(tm // CHUNK_REC):
        sl = slice(cidx * CHUNK_REC, (cidx + 1) * CHUNK_REC)
        lf_c = lf[sl]
        hi = lf_c.astype(BF16)
        rem = lf_c - hi.astype(F32)
        mid = rem.astype(BF16)
        lo = (rem - mid.astype(F32)).astype(BF16)
        pw = jnp.exp(_dot(wsel, jnp.concatenate([hi, mid, lo], axis=0)))
        f_c = jnp.exp(lf_c)
        for hh in range(H_REC):
            hs = slice(hh * DK_REC, (hh + 1) * DK_REC)
            q_h = q[sl, hs]
            k_h = kk[sl, hs]
            i_h = iv[sl, hs]
            k_b = k_h.astype(BF16)
            sc = jnp.where(lev == -1, _dot_nt(q_h.astype(BF16), k_b), 0.0)
            sc = sc + jnp.where(lev == 0, _dot_nt((q_h * f_c[:, hs]).astype(BF16), k_b), 0.0)
            for j in range(nlev - 1):
                qd = (q_h * pw[j * 64:(j + 1) * 64, hs]).astype(BF16)
                kd = (k_h * pw[(nlev + j) * 64:(nlev + j + 1) * 64, hs]).astype(BF16)
                sc = sc + jnp.where(lev == j + 1, _dot_nt(qd, kd), 0.0)
            jl = nlev - 1
            qb = (q_h * pw[jl * 64:(jl + 1) * 64, hs]).astype(BF16)
            kd = (k_h * pw[(nlev + jl) * 64:(nlev + jl + 1) * 64, hs]).astype(BF16)
            st = st_ref[hh]
            o = _dot(sc.astype(BF16), i_h) + _dot_nt(qb, st.astype(BF16))
            dec = pw[jl * 64 + 63:jl * 64 + 64, hs]
            st_ref[hh] = st * dec + _dot_tn(i_h, kd)
            o = o * lax.rsqrt(jnp.mean(o * o, axis=-1, keepdims=True) + EPS) * gain[:, hs]
            ocat_ref[sl, hs] = (o * gate[sl, hs]).astype(BF16)

    zu = proj(4)
    zv = proj(5)
    r128 = lax.broadcasted_iota(I32, (SGU_CHUNK, SGU_CHUNK), 0)
    c128 = lax.broadcasted_iota(I32, (SGU_CHUNK, SGU_CHUNK), 1)
    wtri = [jnp.where(c128 <= r128, ws_ref[hh], 0.0).astype(BF16) for hh in range(H_SGU)]
    for cidx in range(tm // SGU_CHUNK):
        sl = slice(cidx * SGU_CHUNK, (cidx + 1) * SGU_CHUNK)
        ua = _gelu_tanh(zu[sl])
        va = _gelu_tanh(zv[sl])
        mu = jnp.mean(va, axis=-1, keepdims=True)
        vc = va - mu
        var = jnp.mean(vc * vc, axis=-1, keepdims=True)
        vn = (vc * lax.rsqrt(var + EPS) * lng_ref[...] + lnb_ref[...]).astype(BF16)
        for hh in range(H_SGU):
            hs = slice(hh * 128, (hh + 1) * 128)
            s = _dot(wtri[hh], vn[:, hs]) + sb_ref[:, hs]
            ocat_ref[sl, D_REC + hh * 128:D_REC + (hh + 1) * 128] = (ua[:, hs] * s).astype(BF16)

    o_ref[...] = h + _dot(ocat_ref[...], wout_ref[...])


def _mixer(layer, h, gmix, win, la, l1, oml, gain, lng, lnb, ws, sb, wout, wsel, lev):
    t, d = h.shape
    tm = min(TM_MIX, t)
    per_layer = (gmix, win, la, l1, oml, gain, lng, lnb, ws, sb, wout)
    return pl.pallas_call(
        _mixer_kernel,
        grid=(t // tm,),
        in_specs=[pl.BlockSpec((tm, d), lambda i: (i, 0))]
        + [_layer_spec(a, layer) for a in per_layer]
        + [pl.BlockSpec(wsel.shape, lambda i: (0, 0)), pl.BlockSpec(lev.shape, lambda i: (0, 0))],
        out_specs=pl.BlockSpec((tm, d), lambda i: (i, 0)),
        out_shape=jax.ShapeDtypeStruct((t, d), F32),
        scratch_shapes=[pltpu.VMEM((H_REC, DK_REC, DK_REC), F32), pltpu.VMEM((tm, d), BF16)],
        compiler_params=pltpu.CompilerParams(dimension_semantics=("arbitrary",), vmem_limit_bytes=VMEM_LIMIT),
        name="mixer",
    )(h, *per_layer, wsel, lev)


def _stage_selectors(meta, so_row):
    tm = meta.shape[0]
    lane = lax.broadcasted_iota(I32, (tm, ROUTER_LANES), 1).astype(F32)
    col = lax.broadcasted_iota(I32, (tm, STAGE_ROWS), 1)
    sels = []
    for k in range(TOP_K):
        run0 = jnp.sum(jnp.where(lane == meta[:, 2 + k:3 + k], so_row, 0.0), axis=-1, keepdims=True)
        pos = (run0 + meta[:, 4 + k:5 + k]).astype(I32)
        sels.append(col == pos)
    return sels


def _router_kernel(h_ref, g_ref, wr_ref, br_ref, xs_ref, meta_ref, tab_ref):
    tm = h_ref.shape[0]
    xn = _rms(h_ref[...], g_ref[...])
    xh = xn.astype(BF16)
    xl = (xn - xh.astype(F32)).astype(BF16)
    part = _dot(xh, wr_ref[...])
    logits = part[:, :ROUTER_LANES] + part[:, ROUTER_LANES:] + _dot(xl, wr_ref[:, :ROUTER_LANES])
    biased = logits + br_ref[...]
    lane = lax.broadcasted_iota(I32, (tm, ROUTER_LANES), 1).astype(F32)
    ninf = -jnp.inf
    big = float(ROUTER_LANES)

    def first_argmax(v):
        m = jnp.max(v, axis=-1, keepdims=True)
        return jnp.min(jnp.where(v == m, lane, big), axis=-1, keepdims=True)

    def pick(v, idx):
        return jnp.sum(jnp.where(lane == idx, v, 0.0), axis=-1, keepdims=True)

    is_g = lane < float(N_GROUPS)
    g_sel = first_argmax(jnp.where(is_g, biased, ninf))
    lg = jnp.where(is_g, logits, ninf)
    eg = jnp.exp(lg - jnp.max(lg, axis=-1, keepdims=True))
    g_w = pick(eg, g_sel) / jnp.sum(eg, axis=-1, keepdims=True)

    lo = float(EXPERT_LANE0) + float(EXP_PER_GROUP) * g_sel
    is_e = (lane >= lo) & (lane < lo + float(EXP_PER_GROUP))
    eb = jnp.where(is_e, biased, ninf)
    i1 = first_argmax(eb)
    i2 = first_argmax(jnp.where(lane == i1, ninf, eb))
    le = jnp.where(is_e, logits, ninf)
    ee = jnp.exp(le - jnp.max(le, axis=-1, keepdims=True))
    den = jnp.sum(ee, axis=-1, keepdims=True)
    p1 = pick(ee, i1) / den
    p2 = pick(ee, i2) / den
    w1 = g_w * (p1 / (p1 + p2))
    w2 = g_w * (p2 / (p1 + p2))

    hot1 = lane == i1
    hot2 = lane == i2
    cnt = jnp.where(hot1 | hot2, 1.0, 0.0)
    row = lax.broadcasted_iota(I32, (tm, tm), 0)
    col = lax.broadcasted_iota(I32, (tm, tm), 1)
    ltri = jnp.where(col < row, 1.0, 0.0).astype(BF16)
    rank = _dot(ltri, cnt.astype(BF16))
    r1 = jnp.sum(jnp.where(hot1, rank, 0.0), axis=-1, keepdims=True)
    r2 = jnp.sum(jnp.where(hot2, rank, 0.0), axis=-1, keepdims=True)

    n_row = jnp.sum(cnt, axis=0, keepdims=True)
    chunks = jnp.floor((n_row + float(SUBLANES - 1)) * (1.0 / SUBLANES))
    lrow = lax.broadcasted_iota(I32, (ROUTER_LANES, ROUTER_LANES), 0)
    lcol = lax.broadcasted_iota(I32, (ROUTER_LANES, ROUTER_LANES), 1)
    before = jnp.where(lrow < lcol, 1.0, 0.0).astype(BF16)
    chunks8 = jnp.broadcast_to(chunks, (SUBLANES, ROUTER_LANES))
    so_rows = _dot(chunks8.astype(BF16), before) * float(SUBLANES)
    srow = lax.broadcasted_iota(I32, (SUBLANES, ROUTER_LANES), 0)
    tab_ref[...] = jnp.where(srow == 0, jnp.broadcast_to(n_row, (SUBLANES, ROUTER_LANES)), so_rows)

    meta = jnp.where(lane == 0.0, w1, 0.0)
    for j, v in enumerate((w2, i1, i2, r1, r2), start=1):
        meta = jnp.where(lane == float(j), v, meta)
    meta_ref[...] = meta

    s1, s2 = _stage_selectors(meta, so_rows[0:1])
    sel = jnp.where(s1 | s2, 1.0, 0.0).astype(BF16)
    xs_ref[...] = _pack_pairs(_dot_tn(sel, xh))


def _router(layer, h, g, wr, br):
    t, d = h.shape
    tm = TM_MOE
    nt = t // tm
    return pl.pallas_call(
        _router_kernel,
        grid=(nt,),
        in_specs=[pl.BlockSpec((tm, d), lambda i: (i, 0))] + [_layer_spec(a, layer) for a in (g, wr, br)],
        out_specs=[
            pl.BlockSpec((STAGE_ROWS, d // 2), lambda i: (i, 0)),
            pl.BlockSpec((tm, ROUTER_LANES), lambda i: (i, 0)),
            pl.BlockSpec((SUBLANES, ROUTER_LANES), lambda i: (i, 0)),
        ],
        out_shape=[
            jax.ShapeDtypeStruct((nt * STAGE_ROWS, d // 2), I32),
            jax.ShapeDtypeStruct((t, ROUTER_LANES), F32),
            jax.ShapeDtypeStruct((nt * SUBLANES, ROUTER_LANES), F32),
        ],
        compiler_params=pltpu.CompilerParams(dimension_semantics=("arbitrary",), vmem_limit_bytes=VMEM_LIMIT),
        name="router",
    )(h, g, wr, br)


def _chunk_table_kernel(nch_ref, src_ref, dst_ref, ctab_ref, ytab_ref):
    def fill(ref, value):
        def body(c, carry):
            ref[c] = value
            return carry
        lax.fori_loop(0, ref.shape[0], body, 0, unroll=8)

    fill(ctab_ref, ZERO_CHUNK)
    fill(ytab_ref, 0)

    def run(r, carry):
        src = src_ref[r]
        dst = dst_ref[r]

        def one(c, c2):
            ctab_ref[dst + c] = src + c
            ytab_ref[src + c] = dst + c
            return c2

        lax.fori_loop(0, nch_ref[r], one, 0)
        return carry

    lax.fori_loop(0, nch_ref.shape[0], run, 0)


def _chunk_tables(nch, src, dst, n_expert_chunks, n_stage_chunks):
    smem = pl.BlockSpec(memory_space=pltpu.SMEM)
    return pl.pallas_call(
        _chunk_table_kernel,
        in_specs=[smem] * 3,
        out_specs=[smem, smem],
        out_shape=[jax.ShapeDtypeStruct((n_expert_chunks,), I32), jax.ShapeDtypeStruct((n_stage_chunks,), I32)],
        name="chunks",
    )(nch, src, dst)


def _expert_kernel(be_ref, nu_ref, tab_ref, xs_hbm, wg_ref, wu_ref, wd_ref, y_ref, xbuf, wgb, wub, wdb, sem):
    b = pl.program_id(0)
    nu = nu_ref[0]
    bm = xbuf.shape[1]
    per_block = bm // SUBLANES

    def start_gather(blk):
        slot = blk % 2
        for c in range(per_block):
            src = pl.multiple_of(tab_ref[blk * per_block + c] * SUBLANES, SUBLANES)
            pltpu.make_async_copy(xs_hbm.at[pl.ds(src, SUBLANES)],
                                  xbuf.at[slot, pl.ds(c * SUBLANES, SUBLANES)], sem.at[slot]).start()

    def wait_gather(blk):
        slot = blk % 2
        pltpu.make_async_copy(xs_hbm.at[pl.ds(0, bm)], xbuf.at[slot], sem.at[slot]).wait()

    @pl.when(b == 0)
    def _():
        start_gather(0)

    @pl.when(b + 1 < nu)
    def _():
        start_gather(b + 1)

    @pl.when(b < nu)
    def _():
        e = be_ref[b]
        changed = (b == 0) | (e != be_ref[jnp.maximum(b - 1, 0)])

        @pl.when(changed)
        def _():
            wgb[...] = wg_ref[...].astype(BF16)
            wub[...] = wu_ref[...].astype(BF16)
            wdb[...] = wd_ref[...].astype(BF16)

        wait_gather(b)
        x = _unpack_pairs(xbuf[b % 2])
        a = _dot(x, wgb[...])
        u = _dot(x, wub[...])
        hmid = (a * _sigmoid(a)) * u
        y = _dot(hmid.astype(BF16), wdb[...])
        y_ref[...] = _pack_pairs(y.astype(BF16).astype(F32))

    @pl.when(b >= nu)
    def _():
        y_ref[...] = jnp.zeros_like(y_ref)


def _experts(layer, block_e, n_used, tab, xs, wg, wu, wd):
    d, de = wg.shape[-2:]
    dp = xs.shape[1]
    nb = block_e.shape[0]
    wmap = lambda b, be, nu, tb: (layer, be[b], 0, 0)
    return pl.pallas_call(
        _expert_kernel,
        grid_spec=pltpu.PrefetchScalarGridSpec(
            num_scalar_prefetch=3,
            grid=(nb,),
            in_specs=[
                pl.BlockSpec(memory_space=pl.ANY),
                pl.BlockSpec((None, None, d, de), wmap),
                pl.BlockSpec((None, None, d, de), wmap),
                pl.BlockSpec((None, None, de, d), wmap),
            ],
            out_specs=pl.BlockSpec((BM_EXP, dp), lambda b, be, nu, tb: (b, 0)),
            scratch_shapes=[
                pltpu.VMEM((2, BM_EXP, dp), I32),
                pltpu.VMEM((d, de), BF16), pltpu.VMEM((d, de), BF16), pltpu.VMEM((de, d), BF16),
                pltpu.SemaphoreType.DMA((2,)),
            ],
        ),
        out_shape=jax.ShapeDtypeStruct((nb * BM_EXP, dp), I32),
        compiler_params=pltpu.CompilerParams(dimension_semantics=("arbitrary",), vmem_limit_bytes=VMEM_LIMIT),
        name="experts",
    )(block_e, n_used, tab, xs, wg, wu, wd)


def _combine_kernel(ytab_ref, h_ref, meta_ref, tab_ref, p_ref, wpe_ref, npe_ref, npg_ref,
                    wpg_ref, nf_ref, y_hbm, o_ref, stage, sem, *, final):
    i = pl.program_id(0)

    def start_gather(tile):
        slot = tile % 2
        for c in range(STAGE_CHUNKS):
            src = pl.multiple_of(ytab_ref[tile * STAGE_CHUNKS + c] * SUBLANES, SUBLANES)
            pltpu.make_async_copy(y_hbm.at[pl.ds(src, SUBLANES)],
                                  stage.at[slot, pl.ds(c * SUBLANES, SUBLANES)], sem.at[slot]).start()

    @pl.when(i == 0)
    def _():
        start_gather(0)

    @pl.when(i + 1 < pl.num_programs(0))
    def _():
        start_gather(i + 1)

    emb = _rms(_dot(p_ref[...].astype(BF16), wpe_ref[...]), npe_ref[...])

    slot = i % 2
    pltpu.make_async_copy(y_hbm.at[pl.ds(0, STAGE_ROWS)], stage.at[slot], sem.at[slot]).wait()

    meta = meta_ref[...]
    s1, s2 = _stage_selectors(meta, tab_ref[1:2, :])
    sel = (jnp.where(s1, meta[:, 0:1], 0.0) + jnp.where(s2, meta[:, 1:2], 0.0)).astype(BF16)
    h2 = h_ref[...] + _dot(sel, _unpack_pairs(stage[slot]))
    gate = _sigmoid(_dot(_rms(h2, npg_ref[...]).astype(BF16), wpg_ref[...]))
    out = h2 + gate * emb
    if final:
        out = _rms(out, nf_ref[...])
    o_ref[...] = out


def _combine(layer, ytab, h, meta, tab, p, wpe, npe, npg, wpg, nf, y, final):
    t, d = h.shape
    tm = TM_MOE
    tile = lambda r, w: pl.BlockSpec((r, w), lambda i, *s: (i, 0))
    return pl.pallas_call(
        functools.partial(_combine_kernel, final=final),
        grid_spec=pltpu.PrefetchScalarGridSpec(
            num_scalar_prefetch=1,
            grid=(t // tm,),
            in_specs=[
                tile(tm, d), tile(tm, ROUTER_LANES), tile(SUBLANES, ROUTER_LANES),
                pl.BlockSpec((None, None, tm, p.shape[-1]), lambda i, *s: (layer, 0, i, 0)),
                _layer_spec(wpe, layer), _layer_spec(npe, layer), _layer_spec(npg, layer), _layer_spec(wpg, layer),
                pl.BlockSpec(nf.shape, lambda i, *s: (0, 0)),
                pl.BlockSpec(memory_space=pl.ANY),
            ],
            out_specs=tile(tm, d),
            scratch_shapes=[pltpu.VMEM((2, STAGE_ROWS, y.shape[1]), I32), pltpu.SemaphoreType.DMA((2,))],
        ),
        out_shape=jax.ShapeDtypeStruct((t, d), F32),
        compiler_params=pltpu.CompilerParams(dimension_semantics=("arbitrary",), vmem_limit_bytes=VMEM_LIMIT),
        name="combine",
    )(ytab, h, meta, tab, p, wpe, npe, npg, wpg, nf, y)


def _routing_tables(tab, t):
    nt = t // TM_MOE
    tab = tab.reshape(nt, SUBLANES, ROUTER_LANES)[:, :, EXPERT_LANE0:EXPERT_LANE0 + N_EXPERTS]
    counts = tab[:, 0].astype(I32)
    stage_chunk = tab[:, 1].astype(I32) // SUBLANES
    nch = (counts + SUBLANES - 1) // SUBLANES
    per_block = BM_EXP // SUBLANES
    seg = jnp.sum(nch, axis=0)
    padded = (seg + per_block - 1) // per_block * per_block
    pend = jnp.cumsum(padded)
    pstart = pend - padded
    run_chunk = pstart[None, :] + jnp.cumsum(nch, axis=0) - nch
    nb = -(-(TOP_K * t + (SUBLANES - 1) * nt * N_EXPERTS) // BM_EXP) + N_EXPERTS
    bstart = jnp.arange(nb, dtype=I32) * per_block
    block_e = jnp.minimum(jnp.sum((pend[None, :] <= bstart[:, None]).astype(I32), axis=1), N_EXPERTS - 1)
    n_used = pend[-1:] // per_block
    tile_chunk0 = jnp.arange(nt, dtype=I32)[:, None] * STAGE_CHUNKS
    flat = lambda a: a.reshape(-1).astype(I32)
    return flat(nch), flat(tile_chunk0 + stage_chunk), flat(run_chunk), block_e.astype(I32), n_used.astype(I32), nb


def kernel(x, p, norm_mix, w_in, lb_logits, rec_out_gain, sgu_ln_g, sgu_ln_b, sgu_w, sgu_b, w_out, norm_ffn, w_rg, b_rg, w_re, b_re, w_gate, w_up, w_down, norm_pg, w_pg, w_pe, norm_pe, norm_f):
    bsz, seq, d = x.shape
    depth = w_in.shape[0]
    assert bsz == 1 and seq % TM_MOE == 0 and seq % TM_MIX == 0
    h = x.reshape(seq, d)

    lb = jnp.cumsum(jax.nn.softmax(lb_logits.astype(F32), axis=0), axis=0)
    lb = lb - lb[0]
    rows = lambda v: v.reshape(depth, 1, -1)
    la, l1, oml = rows(jnp.log(lb)), rows(jnp.log1p(-lb)), rows(1.0 - lb)
    gain = rows(jnp.tile(rec_out_gain, (1, H_REC)))
    sb = jnp.repeat(jnp.swapaxes(sgu_b, 1, 2), D_SGU // H_SGU, axis=2)
    win_b, wout_b, wpg_b, wpe_b = (w.astype(BF16) for w in (w_in, w_out, w_pg, w_pe))
    npad = ROUTER_LANES - N_GROUPS - N_EXPERTS
    wr = jnp.concatenate([w_rg, w_re, jnp.zeros((depth, d, npad), F32)], axis=2)
    wr_hi = wr.astype(BF16)
    wr = jnp.concatenate([wr_hi, (wr - wr_hi.astype(F32)).astype(BF16)], axis=2)
    br = rows(jnp.concatenate([b_rg, b_re, jnp.zeros((depth, npad), F32)], axis=1))
    wsel = jnp.asarray(np.tile(_decay_selectors(), (1, 3)), BF16)
    lev = jnp.asarray(_pair_levels())
    gmix, gffn, gpg, gpe, lng, lnb = (rows(v) for v in (norm_mix, norm_ffn, norm_pg, norm_pe, sgu_ln_g, sgu_ln_b))
    nf = norm_f.reshape(1, d)

    for l in range(depth):
        h = _mixer(l, h, gmix, win_b, la, l1, oml, gain, lng, lnb, sgu_w, sb, wout_b, wsel, lev)
        xs, meta, tab = _router(l, h, gffn, wr, br)
        nch, xs_chunk, run_chunk, block_e, n_used, nb = _routing_tables(tab, seq)
        ctab, ytab = _chunk_tables(nch, xs_chunk, run_chunk, nb * (BM_EXP // SUBLANES),
                                   (seq // TM_MOE) * STAGE_CHUNKS)
        y = _experts(l, block_e, n_used, ctab, xs, w_gate, w_up, w_down)
        h = _combine(l, ytab, h, meta, tab, p, wpe_b, gpe, gpg, wpg_b, nf, y, final=(l == depth - 1))
    return h.reshape(bsz, seq, d)
```

```python
import functools

import numpy as np
import jax
import jax.numpy as jnp
from jax import lax
from jax.experimental import pallas as pl
from jax.experimental.pallas import tpu as pltpu

F32 = jnp.float32
BF16 = jnp.bfloat16
I32 = jnp.int32

EPS = 1e-6
D_REC = 512
D_SGU = 512
H_REC = 4
DK_REC = 128
CHUNK_REC = 64
SGU_CHUNK = 128
H_SGU = 4
N_GROUPS = 4
EXP_PER_GROUP = 8
N_EXPERTS = N_GROUPS * EXP_PER_GROUP
TOP_K = 2
ROUTER_LANES = 128
EXPERT_LANE0 = N_GROUPS
SUBLANES = 8

TM_MIX = 512
TM_MOE = 256
STAGE_ROWS = 768
BM_EXP = 512
VMEM_LIMIT = 48 * 1024 * 1024

assert STAGE_ROWS >= TOP_K * TM_MOE + (SUBLANES - 1) * N_EXPERTS + SUBLANES
STAGE_CHUNKS = STAGE_ROWS // SUBLANES
ZERO_CHUNK = STAGE_CHUNKS - 1
HIGH_HALF = -65536

_LEVELS = (2, 4, 8, 16, 32, 64)


def _decay_selectors():
    t = np.arange(CHUNK_REC)[:, None]
    s = np.arange(CHUNK_REC)[None, :]
    mats = [((t // m == s // m) & (s <= t)) for m in _LEVELS]
    mats += [((t // m == s // m) & (s > t)) for m in _LEVELS]
    return np.concatenate(mats, axis=0).astype(np.float32)


def _pair_levels():
    t = np.arange(CHUNK_REC)[:, None]
    s = np.arange(CHUNK_REC)[None, :]
    x = t ^ s
    lev = np.floor(np.log2(np.maximum(x, 1))).astype(np.int32)
    lev = np.where(t == s, -1, lev)
    lev = np.where(s > t, -2, lev)
    return lev.astype(np.int32)


def _rms(x, g):
    return x * lax.rsqrt(jnp.mean(x * x, axis=-1, keepdims=True) + EPS) * g


def _sigmoid(x):
    return 1.0 / (1.0 + jnp.exp(-x))


def _gelu_tanh(x):
    return x * (0.5 * (1.0 + jnp.tanh(0.7978845608028654 * (x + 0.044715 * (x * x * x)))))


def _dot(a, b):
    return jnp.dot(a, b, preferred_element_type=F32)


def _dot_nt(a, b):
    return lax.dot_general(a, b, (((1,), (1,)), ((), ())), preferred_element_type=F32)


def _dot_tn(a, b):
    return lax.dot_general(a, b, (((0,), (0,)), ((), ())), preferred_element_type=F32)


def _pack_pairs(x):
    bits = lax.bitcast_convert_type(x, I32)
    half = x.shape[1] // 2
    return lax.shift_right_logical(bits[:, :half], 16) | (bits[:, half:] & HIGH_HALF)


def _unpack_pairs(w):
    lo = lax.bitcast_convert_type(lax.shift_left(w, 16), F32)
    hi = lax.bitcast_convert_type(w & HIGH_HALF, F32)
    return jnp.concatenate([lo, hi], axis=1).astype(BF16)


def _layer_spec(arr, layer):
    zeros = (0,) * (arr.ndim - 1)
    return pl.BlockSpec((None,) + arr.shape[1:], lambda *a: (layer,) + zeros)


def _mixer_kernel(h_ref, gmix_ref, win_ref, la_ref, l1_ref, oml_ref, gain_ref, lng_ref, lnb_ref,
                  ws_ref, sb_ref, wout_ref, wsel_ref, lev_ref, o_ref, st_ref, ocat_ref):
    tm = h_ref.shape[0]

    @pl.when(pl.program_id(0) == 0)
    def _():
        st_ref[...] = jnp.zeros_like(st_ref)

    h = h_ref[...]
    hb = _rms(h, gmix_ref[...]).astype(BF16)

    def proj(j):
        return _dot(hb, win_ref[:, j * 512:(j + 1) * 512])

    zq = proj(0)
    q = zq * _sigmoid(zq)
    fz = proj(1)
    iv = proj(2).astype(BF16)
    zg = proj(3)
    gate = zg * _sigmoid(zg)

    ls = jnp.minimum(fz, 0.0) - jnp.log1p(jnp.exp(-jnp.abs(fz)))
    c = l1_ref[...] + ls
    la = la_ref[...]
    lf = jnp.maximum(la, c) + jnp.log1p(jnp.exp(-jnp.abs(la - c)))
    kk = oml_ref[...] * (1.0 / (1.0 + jnp.exp(fz)))

    wsel = wsel_ref[...]
    lev = lev_ref[...]
    gain = gain_ref[...]
    nlev = len(_LEVELS)

    for cidx in range(tm // CHUNK_REC):
        sl = slice(cidx * CHUNK_REC, (cidx + 1) * CHUNK_REC)
        lf_c = lf[sl]
        hi = lf_c.astype(BF16)
        rem = lf_c - hi.astype(F32)
        mid = rem.astype(BF16)
        lo = (rem - mid.astype(F32)).astype(BF16)
        pw = jnp.exp(_dot(wsel, jnp.concatenate([hi, mid, lo], axis=0)))
        f_c = jnp.exp(lf_c)
        for hh in range(H_REC):
            hs = slice(hh * DK_REC, (hh + 1) * DK_REC)
            q_h = q[sl, hs]
            k_h = kk[sl, hs]
            i_h = iv[sl, hs]
            k_b = k_h.astype(BF16)
            sc = jnp.where(lev == -1, _dot_nt(q_h.astype(BF16), k_b), 0.0)
            sc = sc + jnp.where(lev == 0, _dot_nt((q_h * f_c[:, hs]).astype(BF16), k_b), 0.0)
            for j in range(nlev - 1):
                qd = (q_h * pw[j * 64:(j + 1) * 64, hs]).astype(BF16)
                kd = (k_h * pw[(nlev + j) * 64:(nlev + j + 1) * 64, hs]).astype(BF16)
                sc = sc + jnp.where(lev == j + 1, _dot_nt(qd, kd), 0.0)
            jl = nlev - 1
            qb = (q_h * pw[jl * 64:(jl + 1) * 64, hs]).astype(BF16)
            kd = (k_h * pw[(nlev + jl) * 64:(nlev + jl + 1) * 64, hs]).astype(BF16)
            st = st_ref[hh]
            o = _dot(sc.astype(BF16), i_h) + _dot_nt(qb, st.astype(BF16))
            dec = pw[jl * 64 + 63:jl * 64 + 64, hs]
            st_ref[hh] = st * dec + _dot_tn(i_h, kd)
            o = o * lax.rsqrt(jnp.mean(o * o, axis=-1, keepdims=True) + EPS) * gain[:, hs]
            ocat_ref[sl, hs] = (o * gate[sl, hs]).astype(BF16)

    zu = proj(4)
    zv = proj(5)
    r128 = lax.broadcasted_iota(I32, (SGU_CHUNK, SGU_CHUNK), 0)
    c128 = lax.broadcasted_iota(I32, (SGU_CHUNK, SGU_CHUNK), 1)
    wtri = [jnp.where(c128 <= r128, ws_ref[hh], 0.0).astype(BF16) for hh in range(H_SGU)]
    for cidx in range(tm // SGU_CHUNK):
        sl = slice(cidx * SGU_CHUNK, (cidx + 1) * SGU_CHUNK)
        ua = _gelu_tanh(zu[sl])
        va = _gelu_tanh(zv[sl])
        mu = jnp.mean(va, axis=-1, keepdims=True)
        vc = va - mu
        var = jnp.mean(vc * vc, axis=-1, keepdims=True)
        vn = (vc * lax.rsqrt(var + EPS) * lng_ref[...] + lnb_ref[...]).astype(BF16)
        for hh in range(H_SGU):
            hs = slice(hh * 128, (hh + 1) * 128)
            s = _dot(wtri[hh], vn[:, hs]) + sb_ref[:, hs]
            ocat_ref[sl, D_REC + hh * 128:D_REC + (hh + 1) * 128] = (ua[:, hs] * s).astype(BF16)

    o_ref[...] = h + _dot(ocat_ref[...], wout_ref[...])


def _mixer(layer, h, gmix, win, la, l1, oml, gain, lng, lnb, ws, sb, wout, wsel, lev):
    t, d = h.shape
    tm = min(TM_MIX, t)
    per_layer = (gmix, win, la, l1, oml, gain, lng, lnb, ws, sb, wout)
    return pl.pallas_call(
        _mixer_kernel,
        grid=(t // tm,),
        in_specs=[pl.BlockSpec((tm, d), lambda i: (i, 0))]
        + [_layer_spec(a, layer) for a in per_layer]
        + [pl.BlockSpec(wsel.shape, lambda i: (0, 0)), pl.BlockSpec(lev.shape, lambda i: (0, 0))],
        out_specs=pl.BlockSpec((tm, d), lambda i: (i, 0)),
        out_shape=jax.ShapeDtypeStruct((t, d), F32),
        scratch_shapes=[pltpu.VMEM((H_REC, DK_REC, DK_REC), F32), pltpu.VMEM((tm, d), BF16)],
        compiler_params=pltpu.CompilerParams(dimension_semantics=("arbitrary",), vmem_limit_bytes=VMEM_LIMIT),
        name="mixer",
    )(h, *per_layer, wsel, lev)


def _stage_selectors(meta, so_row):
    tm = meta.shape[0]
    lane = lax.broadcasted_iota(I32, (tm, ROUTER_LANES), 1).astype(F32)
    col = lax.broadcasted_iota(I32, (tm, STAGE_ROWS), 1)
    sels = []
    for k in range(TOP_K):
        run0 = jnp.sum(jnp.where(lane == meta[:, 2 + k:3 + k], so_row, 0.0), axis=-1, keepdims=True)
        pos = (run0 + meta[:, 4 + k:5 + k]).astype(I32)
        sels.append(col == pos)
    return sels


def _router_kernel(h_ref, g_ref, wr_ref, br_ref, xs_ref, meta_ref, tab_ref):
    tm = h_ref.shape[0]
    xn = _rms(h_ref[...], g_ref[...])
    xh = xn.astype(BF16)
    xl = (xn - xh.astype(F32)).astype(BF16)
    part = _dot(xh, wr_ref[...])
    logits = part[:, :ROUTER_LANES] + part[:, ROUTER_LANES:] + _dot(xl, wr_ref[:, :ROUTER_LANES])
    biased = logits + br_ref[...]
    lane = lax.broadcasted_iota(I32, (tm, ROUTER_LANES), 1).astype(F32)
    ninf = -jnp.inf
    big = float(ROUTER_LANES)

    def first_argmax(v):
        m = jnp.max(v, axis=-1, keepdims=True)
        return jnp.min(jnp.where(v == m, lane, big), axis=-1, keepdims=True)

    def pick(v, idx):
        return jnp.sum(jnp.where(lane == idx, v, 0.0), axis=-1, keepdims=True)

    is_g = lane < float(N_GROUPS)
    g_sel = first_argmax(jnp.where(is_g, biased, ninf))
    lg = jnp.where(is_g, logits, ninf)
    eg = jnp.exp(lg - jnp.max(lg, axis=-1, keepdims=True))
    g_w = pick(eg, g_sel) / jnp.sum(eg, axis=-1, keepdims=True)

    lo = float(EXPERT_LANE0) + float(EXP_PER_GROUP) * g_sel
    is_e = (lane >= lo) & (lane < lo + float(EXP_PER_GROUP))
    eb = jnp.where(is_e, biased, ninf)
    i1 = first_argmax(eb)
    i2 = first_argmax(jnp.where(lane == i1, ninf, eb))
    le = jnp.where(is_e, logits, ninf)
    ee = jnp.exp(le - jnp.max(le, axis=-1, keepdims=True))
    den = jnp.sum(ee, axis=-1, keepdims=True)
    p1 = pick(ee, i1) / den
    p2 = pick(ee, i2) / den
    w1 = g_w * (p1 / (p1 + p2))
    w2 = g_w * (p2 / (p1 + p2))

    hot1 = lane == i1
    hot2 = lane == i2
    cnt = jnp.where(hot1 | hot2, 1.0, 0.0)
    row = lax.broadcasted_iota(I32, (tm, tm), 0)
    col = lax.broadcasted_iota(I32, (tm, tm), 1)
    ltri = jnp.where(col < row, 1.0, 0.0).astype(BF16)
    rank = _dot(ltri, cnt.astype(BF16))
    r1 = jnp.sum(jnp.where(hot1, rank, 0.0), axis=-1, keepdims=True)
    r2 = jnp.sum(jnp.where(hot2, rank, 0.0), axis=-1, keepdims=True)

    n_row = jnp.sum(cnt, axis=0, keepdims=True)
    chunks = jnp.floor((n_row + float(SUBLANES - 1)) * (1.0 / SUBLANES))
    lrow = lax.broadcasted_iota(I32, (ROUTER_LANES, ROUTER_LANES), 0)
    lcol = lax.broadcasted_iota(I32, (ROUTER_LANES, ROUTER_LANES), 1)
    before = jnp.where(lrow < lcol, 1.0, 0.0).astype(BF16)
    chunks8 = jnp.broadcast_to(chunks, (SUBLANES, ROUTER_LANES))
    so_rows = _dot(chunks8.astype(BF16), before) * float(SUBLANES)
    srow = lax.broadcasted_iota(I32, (SUBLANES, ROUTER_LANES), 0)
    tab_ref[...] = jnp.where(srow == 0, jnp.broadcast_to(n_row, (SUBLANES, ROUTER_LANES)), so_rows)

    meta = jnp.where(lane == 0.0, w1, 0.0)
    for j, v in enumerate((w2, i1, i2, r1, r2), start=1):
        meta = jnp.where(lane == float(j), v, meta)
    meta_ref[...] = meta

    s1, s2 = _stage_selectors(meta, so_rows[0:1])
    sel = jnp.where(s1 | s2, 1.0, 0.0).astype(BF16)
    xs_ref[...] = _pack_pairs(_dot_tn(sel, xh))


def _router(layer, h, g, wr, br):
    t, d = h.shape
    tm = TM_MOE
    nt = t // tm
    return pl.pallas_call(
        _router_kernel,
        grid=(nt,),
        in_specs=[pl.BlockSpec((tm, d), lambda i: (i, 0))] + [_layer_spec(a, layer) for a in (g, wr, br)],
        out_specs=[
            pl.BlockSpec((STAGE_ROWS, d // 2), lambda i: (i, 0)),
            pl.BlockSpec((tm, ROUTER_LANES), lambda i: (i, 0)),
            pl.BlockSpec((SUBLANES, ROUTER_LANES), lambda i: (i, 0)),
        ],
        out_shape=[
            jax.ShapeDtypeStruct((nt * STAGE_ROWS, d // 2), I32),
            jax.ShapeDtypeStruct((t, ROUTER_LANES), F32),
            jax.ShapeDtypeStruct((nt * SUBLANES, ROUTER_LANES), F32),
        ],
        compiler_params=pltpu.CompilerParams(dimension_semantics=("arbitrary",), vmem_limit_bytes=VMEM_LIMIT),
        name="router",
    )(h, g, wr, br)


def _match_runs(query, key, n, val, default):
    d = query - key
    inside = (d >= 0.0) & (d < n)
    hit = jnp.sum(jnp.where(inside, 1.0, 0.0), axis=0, keepdims=True)
    got = jnp.sum(jnp.where(inside, val + d, 0.0), axis=0, keepdims=True)
    return jnp.where(hit > 0.0, got, float(default)).astype(I32)


def _chunk_table_kernel(be_ref, tkey_ref, tn_ref, tval_ref, ekey_ref, en_ref, eval_ref, ctab_ref, ytab_ref):
    lanes = ytab_ref.shape[1]
    nruns = tkey_ref.shape[0]
    lane = lax.broadcasted_iota(I32, (1, lanes), 1).astype(F32)
    tiles_met = -(-lanes // STAGE_CHUNKS) + 1
    for k in range(ytab_ref.shape[0]):
        r0 = min((k * lanes) // STAGE_CHUNKS * N_EXPERTS, nruns - tiles_met * N_EXPERTS)
        rs = slice(r0, r0 + tiles_met * N_EXPERTS)
        ytab_ref[k:k + 1, :] = _match_runs(lane + float(k * lanes), tkey_ref[rs, :], tn_ref[rs, :],
                                           tval_ref[rs, :], 0)

    per_block = ctab_ref.shape[1]
    per_expert = nruns // N_EXPERTS
    lane_b = lane[:, :per_block]

    def block(b, carry):
        r0 = pl.multiple_of(be_ref[b] * per_expert, per_expert)
        rs = pl.ds(r0, per_expert)
        query = lane_b + lax.convert_element_type(b * per_block, F32)
        ctab_ref[pl.ds(b, 1), :] = _match_runs(query, ekey_ref[rs, :per_block], en_ref[rs, :per_block],
                                               eval_ref[rs, :per_block], ZERO_CHUNK)
        return carry

    lax.fori_loop(0, ctab_ref.shape[0], block, 0)


def _chunk_tables(block_e, nch, xs_chunk, run_chunk):
    nt = nch.shape[0]
    nb = block_e.shape[0]
    lanes = ROUTER_LANES
    wide = lambda a: jnp.broadcast_to(a.reshape(-1, 1).astype(F32), (a.size, lanes))
    tile_major = [wide(a) for a in (xs_chunk, nch, run_chunk)]
    expert_major = [wide(a.T) for a in (run_chunk, nch, xs_chunk)]
    full = lambda a: pl.BlockSpec(a.shape, lambda i, be: (0, 0))
    n_stage_rows = nt * STAGE_CHUNKS // lanes
    ctab, ytab = pl.pallas_call(
        _chunk_table_kernel,
        grid_spec=pltpu.PrefetchScalarGridSpec(
            num_scalar_prefetch=1,
            grid=(1,),
            in_specs=[full(a) for a in tile_major + expert_major],
            out_specs=[pl.BlockSpec((nb, BM_EXP // SUBLANES), lambda i, be: (0, 0)),
                       pl.BlockSpec((n_stage_rows, lanes), lambda i, be: (0, 0))],
        ),
        out_shape=[jax.ShapeDtypeStruct((nb, BM_EXP // SUBLANES), I32),
                   jax.ShapeDtypeStruct((n_stage_rows, lanes), I32)],
        name="chunks",
    )(block_e, *tile_major, *expert_major)
    return ctab.reshape(-1), ytab.reshape(-1)


def _expert_kernel(be_ref, nu_ref, tab_ref, xs_hbm, wg_ref, wu_ref, wd_ref, y_ref, xbuf, sem):
    b = pl.program_id(0)
    nu = nu_ref[0]
    bm = xbuf.shape[1]
    per_block = bm // SUBLANES

    def start_gather(blk):
        slot = blk % 2
        for c in range(per_block):
            src = pl.multiple_of(tab_ref[blk * per_block + c] * SUBLANES, SUBLANES)
            pltpu.make_async_copy(xs_hbm.at[pl.ds(src, SUBLANES)],
                                  xbuf.at[slot, pl.ds(c * SUBLANES, SUBLANES)], sem.at[slot]).start()

    def wait_gather(blk):
        slot = blk % 2
        pltpu.make_async_copy(xs_hbm.at[pl.ds(0, bm)], xbuf.at[slot], sem.at[slot]).wait()

    @pl.when(b == 0)
    def _():
        start_gather(0)

    @pl.when(b + 1 < nu)
    def _():
        start_gather(b + 1)

    @pl.when(b < nu)
    def _():
        wait_gather(b)
        x = _unpack_pairs(xbuf[b % 2])
        a = _dot(x, wg_ref[...].astype(BF16))
        u = _dot(x, wu_ref[...].astype(BF16))
        hmid = (a * _sigmoid(a)) * u
        y = _dot(hmid.astype(BF16), wd_ref[...].astype(BF16))
        y_ref[...] = _pack_pairs(y.astype(BF16).astype(F32))

    @pl.when(b >= nu)
    def _():
        y_ref[...] = jnp.zeros_like(y_ref)


def _experts(layer, block_e, n_used, tab, xs, wg, wu, wd):
    d, de = wg.shape[-2:]
    dp = xs.shape[1]
    nb = block_e.shape[0]
    wmap = lambda b, be, nu, tb: (layer, be[b], 0, 0)
    return pl.pallas_call(
        _expert_kernel,
        grid_spec=pltpu.PrefetchScalarGridSpec(
            num_scalar_prefetch=3,
            grid=(nb,),
            in_specs=[
                pl.BlockSpec(memory_space=pl.ANY),
                pl.BlockSpec((None, None, d, de), wmap),
                pl.BlockSpec((None, None, d, de), wmap),
                pl.BlockSpec((None, None, de, d), wmap),
            ],
            out_specs=pl.BlockSpec((BM_EXP, dp), lambda b, be, nu, tb: (b, 0)),
            scratch_shapes=[
                pltpu.VMEM((2, BM_EXP, dp), I32),
                pltpu.SemaphoreType.DMA((2,)),
            ],
        ),
        out_shape=jax.ShapeDtypeStruct((nb * BM_EXP, dp), I32),
        compiler_params=pltpu.CompilerParams(dimension_semantics=("arbitrary",), vmem_limit_bytes=VMEM_LIMIT),
        name="experts",
    )(block_e, n_used, tab, xs, wg, wu, wd)


def _combine_kernel(ytab_ref, h_ref, meta_ref, tab_ref, p_ref, wpe_ref, npe_ref, npg_ref,
                    wpg_ref, nf_ref, y_hbm, o_ref, stage, sem, *, final):
    i = pl.program_id(0)

    def start_gather(tile):
        slot = tile % 2
        for c in range(STAGE_CHUNKS):
            src = pl.multiple_of(ytab_ref[tile * STAGE_CHUNKS + c] * SUBLANES, SUBLANES)
            pltpu.make_async_copy(y_hbm.at[pl.ds(src, SUBLANES)],
                                  stage.at[slot, pl.ds(c * SUBLANES, SUBLANES)], sem.at[slot]).start()

    @pl.when(i == 0)
    def _():
        start_gather(0)

    @pl.when(i + 1 < pl.num_programs(0))
    def _():
        start_gather(i + 1)

    emb = _rms(_dot(p_ref[...].astype(BF16), wpe_ref[...]), npe_ref[...])

    slot = i % 2
    pltpu.make_async_copy(y_hbm.at[pl.ds(0, STAGE_ROWS)], stage.at[slot], sem.at[slot]).wait()

    meta = meta_ref[...]
    s1, s2 = _stage_selectors(meta, tab_ref[1:2, :])
    sel = (jnp.where(s1, meta[:, 0:1], 0.0) + jnp.where(s2, meta[:, 1:2], 0.0)).astype(BF16)
    h2 = h_ref[...] + _dot(sel, _unpack_pairs(stage[slot]))
    gate = _sigmoid(_dot(_rms(h2, npg_ref[...]).astype(BF16), wpg_ref[...]))
    out = h2 + gate * emb
    if final:
        out = _rms(out, nf_ref[...])
    o_ref[...] = out


def _combine(layer, ytab, h, meta, tab, p, wpe, npe, npg, wpg, nf, y, final):
    t, d = h.shape
    tm = TM_MOE
    tile = lambda r, w: pl.BlockSpec((r, w), lambda i, *s: (i, 0))
    return pl.pallas_call(
        functools.partial(_combine_kernel, final=final),
        grid_spec=pltpu.PrefetchScalarGridSpec(
            num_scalar_prefetch=1,
            grid=(t // tm,),
            in_specs=[
                tile(tm, d), tile(tm, ROUTER_LANES), tile(SUBLANES, ROUTER_LANES),
                pl.BlockSpec((None, None, tm, p.shape[-1]), lambda i, *s: (layer, 0, i, 0)),
                _layer_spec(wpe, layer), _layer_spec(npe, layer), _layer_spec(npg, layer), _layer_spec(wpg, layer),
                pl.BlockSpec(nf.shape, lambda i, *s: (0, 0)),
                pl.BlockSpec(memory_space=pl.ANY),
            ],
            out_specs=tile(tm, d),
            scratch_shapes=[pltpu.VMEM((2, STAGE_ROWS, y.shape[1]), I32), pltpu.SemaphoreType.DMA((2,))],
        ),
        out_shape=jax.ShapeDtypeStruct((t, d), F32),
        compiler_params=pltpu.CompilerParams(dimension_semantics=("arbitrary",), vmem_limit_bytes=VMEM_LIMIT),
        name="combine",
    )(ytab, h, meta, tab, p, wpe, npe, npg, wpg, nf, y)


def _routing_tables(tab, t):
    nt = t // TM_MOE
    tab = tab.reshape(nt, SUBLANES, ROUTER_LANES)[:, :, EXPERT_LANE0:EXPERT_LANE0 + N_EXPERTS]
    counts = tab[:, 0].astype(I32)
    stage_chunk = tab[:, 1].astype(I32) // SUBLANES
    nch = (counts + SUBLANES - 1) // SUBLANES
    per_block = BM_EXP // SUBLANES
    seg = jnp.sum(nch, axis=0)
    padded = (seg + per_block - 1) // per_block * per_block
    pend = jnp.cumsum(padded)
    pstart = pend - padded
    run_chunk = pstart[None, :] + jnp.cumsum(nch, axis=0) - nch
    nb = -(-(TOP_K * t + (SUBLANES - 1) * nt * N_EXPERTS) // BM_EXP) + N_EXPERTS
    bstart = jnp.arange(nb, dtype=I32) * per_block
    block_e = jnp.minimum(jnp.sum((pend[None, :] <= bstart[:, None]).astype(I32), axis=1), N_EXPERTS - 1)
    n_used = pend[-1:] // per_block
    tile_chunk0 = jnp.arange(nt, dtype=I32)[:, None] * STAGE_CHUNKS
    return nch, tile_chunk0 + stage_chunk, run_chunk, block_e.astype(I32), n_used.astype(I32)


def kernel(x, p, norm_mix, w_in, lb_logits, rec_out_gain, sgu_ln_g, sgu_ln_b, sgu_w, sgu_b, w_out, norm_ffn, w_rg, b_rg, w_re, b_re, w_gate, w_up, w_down, norm_pg, w_pg, w_pe, norm_pe, norm_f):
    bsz, seq, d = x.shape
    depth = w_in.shape[0]
    assert bsz == 1 and seq % TM_MOE == 0 and seq % TM_MIX == 0
    h = x.reshape(seq, d)

    lb = jnp.cumsum(jax.nn.softmax(lb_logits.astype(F32), axis=0), axis=0)
    lb = lb - lb[0]
    rows = lambda v: v.reshape(depth, 1, -1)
    la, l1, oml = rows(jnp.log(lb)), rows(jnp.log1p(-lb)), rows(1.0 - lb)
    gain = rows(jnp.tile(rec_out_gain, (1, H_REC)))
    sb = jnp.repeat(jnp.swapaxes(sgu_b, 1, 2), D_SGU // H_SGU, axis=2)
    win_b, wout_b, wpg_b, wpe_b = (w.astype(BF16) for w in (w_in, w_out, w_pg, w_pe))
    npad = ROUTER_LANES - N_GROUPS - N_EXPERTS
    wr = jnp.concatenate([w_rg, w_re, jnp.zeros((depth, d, npad), F32)], axis=2)
    wr_hi = wr.astype(BF16)
    wr = jnp.concatenate([wr_hi, (wr - wr_hi.astype(F32)).astype(BF16)], axis=2)
    br = rows(jnp.concatenate([b_rg, b_re, jnp.zeros((depth, npad), F32)], axis=1))
    wsel = jnp.asarray(np.tile(_decay_selectors(), (1, 3)), BF16)
    lev = jnp.asarray(_pair_levels())
    gmix, gffn, gpg, gpe, lng, lnb = (rows(v) for v in (norm_mix, norm_ffn, norm_pg, norm_pe, sgu_ln_g, sgu_ln_b))
    nf = norm_f.reshape(1, d)

    for l in range(depth):
        h = _mixer(l, h, gmix, win_b, la, l1, oml, gain, lng, lnb, sgu_w, sb, wout_b, wsel, lev)
        xs, meta, tab = _router(l, h, gffn, wr, br)
        nch, xs_chunk, run_chunk, block_e, n_used = _routing_tables(tab, seq)
        ctab, ytab = _chunk_tables(block_e, nch, xs_chunk, run_chunk)
        y = _experts(l, block_e, n_used, ctab, xs, w_gate, w_up, w_down)
        h = _combine(l, ytab, h, meta, tab, p, wpe_b, gpe, gpg, wpg_b, nf, y, final=(l == depth - 1))
    return h.reshape(bsz, seq, d)
```

```python
import functools

import numpy as np
import jax
import jax.numpy as jnp
from jax import lax
from jax.experimental import pallas as pl
from jax.experimental.pallas import tpu as pltpu

F32 = jnp.float32
BF16 = jnp.bfloat16
I32 = jnp.int32

EPS = 1e-6
D_REC = 512
D_SGU = 512
H_REC = 4
DK_REC = 128
CHUNK_REC = 64
SGU_CHUNK = 128
H_SGU = 4
N_GROUPS = 4
EXP_PER_GROUP = 8
N_EXPERTS = N_GROUPS * EXP_PER_GROUP
TOP_K = 2
ROUTER_LANES = 128
EXPERT_LANE0 = N_GROUPS
SUBLANES = 8

TM_MIX = 512
TM_MOE = 256
MOE_STEP_TILES = 4
STAGE_ROWS = 768
BM_EXP = 512
VMEM_LIMIT = 48 * 1024 * 1024

assert STAGE_ROWS >= TOP_K * TM_MOE + (SUBLANES - 1) * N_EXPERTS + SUBLANES
STAGE_CHUNKS = STAGE_ROWS // SUBLANES
ZERO_CHUNK = STAGE_CHUNKS - 1
HIGH_HALF = -65536

_LEVELS = (2, 4, 8, 16, 32, 64)


def _decay_selectors():
    t = np.arange(CHUNK_REC)[:, None]
    s = np.arange(CHUNK_REC)[None, :]
    mats = [((t // m == s // m) & (s <= t)) for m in _LEVELS]
    mats += [((t // m == s // m) & (s > t)) for m in _LEVELS]
    return np.concatenate(mats, axis=0).astype(np.float32)


def _pair_levels():
    t = np.arange(CHUNK_REC)[:, None]
    s = np.arange(CHUNK_REC)[None, :]
    x = t ^ s
    lev = np.floor(np.log2(np.maximum(x, 1))).astype(np.int32)
    lev = np.where(t == s, -1, lev)
    lev = np.where(s > t, -2, lev)
    return lev.astype(np.int32)


def _rms(x, g):
    return x * lax.rsqrt(jnp.mean(x * x, axis=-1, keepdims=True) + EPS) * g


def _sigmoid(x):
    return 1.0 / (1.0 + jnp.exp(-x))


def _gelu_tanh(x):
    return x * (0.5 * (1.0 + jnp.tanh(0.7978845608028654 * (x + 0.044715 * (x * x * x)))))


def _dot(a, b):
    return jnp.dot(a, b, preferred_element_type=F32)


def _dot_nt(a, b):
    return lax.dot_general(a, b, (((1,), (1,)), ((), ())), preferred_element_type=F32)


def _dot_tn(a, b):
    return lax.dot_general(a, b, (((0,), (0,)), ((), ())), preferred_element_type=F32)


def _pack_pairs(x):
    bits = lax.bitcast_convert_type(x, I32)
    half = x.shape[1] // 2
    return lax.shift_right_logical(bits[:, :half], 16) | (bits[:, half:] & HIGH_HALF)


def _unpack_pairs(w):
    lo = lax.bitcast_convert_type(lax.shift_left(w, 16), F32)
    hi = lax.bitcast_convert_type(w & HIGH_HALF, F32)
    return jnp.concatenate([lo, hi], axis=1).astype(BF16)


def _layer_spec(arr, layer):
    zeros = (0,) * (arr.ndim - 1)
    return pl.BlockSpec((None,) + arr.shape[1:], lambda *a: (layer,) + zeros)


def _mixer_kernel(h_ref, gmix_ref, win_ref, la_ref, l1_ref, oml_ref, gain_ref, lng_ref, lnb_ref,
                  ws_ref, sb_ref, wout_ref, wsel_ref, lev_ref, o_ref, st_ref, ocat_ref):
    tm = h_ref.shape[0]

    @pl.when(pl.program_id(0) == 0)
    def _():
        st_ref[...] = jnp.zeros_like(st_ref)

    h = h_ref[...]
    hb = _rms(h, gmix_ref[...]).astype(BF16)

    def proj(j):
        return _dot(hb, win_ref[:, j * 512:(j + 1) * 512])

    zq = proj(0)
    q = zq * _sigmoid(zq)
    fz = proj(1)
    iv = proj(2).astype(BF16)
    zg = proj(3)
    gate = zg * _sigmoid(zg)

    ls = jnp.minimum(fz, 0.0) - jnp.log1p(jnp.exp(-jnp.abs(fz)))
    c = l1_ref[...] + ls
    la = la_ref[...]
    lf = jnp.maximum(la, c) + jnp.log1p(jnp.exp(-jnp.abs(la - c)))
    kk = oml_ref[...] * (1.0 / (1.0 + jnp.exp(fz)))

    wsel = wsel_ref[...]
    lev = lev_ref[...]
    gain = gain_ref[...]
    nlev = len(_LEVELS)
    states = [st_ref[hh] for hh in range(H_REC)]

    for cidx in range(tm // CHUNK_REC):
        sl = slice(cidx * CHUNK_REC, (cidx + 1) * CHUNK_REC)
        lf_c = lf[sl]
        hi = lf_c.astype(BF16)
        rem = lf_c - hi.astype(F32)
        mid = rem.astype(BF16)
        lo = (rem - mid.astype(F32)).astype(BF16)
        pw = jnp.exp(_dot(wsel, jnp.concatenate([hi, mid, lo], axis=0)))
        f_c = jnp.exp(lf_c)
        for hh in range(H_REC):
            hs = slice(hh * DK_REC, (hh + 1) * DK_REC)
            q_h = q[sl, hs]
            k_h = kk[sl, hs]
            i_h = iv[sl, hs]
            k_b = k_h.astype(BF16)
            sc = jnp.where(lev == -1, _dot_nt(q_h.astype(BF16), k_b), 0.0)
            sc = sc + jnp.where(lev == 0, _dot_nt((q_h * f_c[:, hs]).astype(BF16), k_b), 0.0)
            for j in range(nlev - 1):
                qd = (q_h * pw[j * 64:(j + 1) * 64, hs]).astype(BF16)
                kd = (k_h * pw[(nlev + j) * 64:(nlev + j + 1) * 64, hs]).astype(BF16)
                sc = sc + jnp.where(lev == j + 1, _dot_nt(qd, kd), 0.0)
            jl = nlev - 1
            qb = (q_h * pw[jl * 64:(jl + 1) * 64, hs]).astype(BF16)
            kd = (k_h * pw[(nlev + jl) * 64:(nlev + jl + 1) * 64, hs]).astype(BF16)
            st = states[hh]
            o = _dot(sc.astype(BF16), i_h) + _dot_nt(qb, st.astype(BF16))
            dec = pw[jl * 64 + 63:jl * 64 + 64, hs]
            states[hh] = st * dec + _dot_tn(i_h, kd)
            o = o * lax.rsqrt(jnp.mean(o * o, axis=-1, keepdims=True) + EPS) * gain[:, hs]
            ocat_ref[sl, hs] = (o * gate[sl, hs]).astype(BF16)

    for hh in range(H_REC):
        st_ref[hh] = states[hh]

    zu = proj(4)
    zv = proj(5)
    r128 = lax.broadcasted_iota(I32, (SGU_CHUNK, SGU_CHUNK), 0)
    c128 = lax.broadcasted_iota(I32, (SGU_CHUNK, SGU_CHUNK), 1)
    wtri = [jnp.where(c128 <= r128, ws_ref[hh], 0.0).astype(BF16) for hh in range(H_SGU)]
    for cidx in range(tm // SGU_CHUNK):
        sl = slice(cidx * SGU_CHUNK, (cidx + 1) * SGU_CHUNK)
        ua = _gelu_tanh(zu[sl])
        va = _gelu_tanh(zv[sl])
        mu = jnp.mean(va, axis=-1, keepdims=True)
        vc = va - mu
        var = jnp.mean(vc * vc, axis=-1, keepdims=True)
        vn = (vc * lax.rsqrt(var + EPS) * lng_ref[...] + lnb_ref[...]).astype(BF16)
        for hh in range(H_SGU):
            hs = slice(hh * 128, (hh + 1) * 128)
            s = _dot(wtri[hh], vn[:, hs]) + sb_ref[:, hs]
            ocat_ref[sl, D_REC + hh * 128:D_REC + (hh + 1) * 128] = (ua[:, hs] * s).astype(BF16)

    o_ref[...] = h + _dot(ocat_ref[...], wout_ref[...])


def _mixer(layer, h, gmix, win, la, l1, oml, gain, lng, lnb, ws, sb, wout, wsel, lev):
    t, d = h.shape
    tm = min(TM_MIX, t)
    per_layer = (gmix, win, la, l1, oml, gain, lng, lnb, ws, sb, wout)
    return pl.pallas_call(
        _mixer_kernel,
        grid=(t // tm,),
        in_specs=[pl.BlockSpec((tm, d), lambda i: (i, 0))]
        + [_layer_spec(a, layer) for a in per_layer]
        + [pl.BlockSpec(wsel.shape, lambda i: (0, 0)), pl.BlockSpec(lev.shape, lambda i: (0, 0))],
        out_specs=pl.BlockSpec((tm, d), lambda i: (i, 0)),
        out_shape=jax.ShapeDtypeStruct((t, d), F32),
        scratch_shapes=[pltpu.VMEM((H_REC, DK_REC, DK_REC), F32), pltpu.VMEM((tm, d), BF16)],
        compiler_params=pltpu.CompilerParams(dimension_semantics=("arbitrary",), vmem_limit_bytes=VMEM_LIMIT),
        name="mixer",
    )(h, *per_layer, wsel, lev)


def _stage_selectors(meta, so_row):
    tm = meta.shape[0]
    lane = lax.broadcasted_iota(I32, (tm, ROUTER_LANES), 1).astype(F32)
    col = lax.broadcasted_iota(I32, (tm, STAGE_ROWS), 1)
    sels = []
    for k in range(TOP_K):
        run0 = jnp.sum(jnp.where(lane == meta[:, 2 + k:3 + k], so_row, 0.0), axis=-1, keepdims=True)
        pos = (run0 + meta[:, 4 + k:5 + k]).astype(I32)
        sels.append(col == pos)
    return sels


def _route_tile(logits, biased, xh):
    tm = logits.shape[0]
    lane = lax.broadcasted_iota(I32, (tm, ROUTER_LANES), 1).astype(F32)
    ninf = -jnp.inf
    big = float(ROUTER_LANES)

    def first_argmax(v):
        m = jnp.max(v, axis=-1, keepdims=True)
        return jnp.min(jnp.where(v == m, lane, big), axis=-1, keepdims=True)

    def pick(v, idx):
        return jnp.sum(jnp.where(lane == idx, v, 0.0), axis=-1, keepdims=True)

    is_g = lane < float(N_GROUPS)
    g_sel = first_argmax(jnp.where(is_g, biased, ninf))
    lg = jnp.where(is_g, logits, ninf)
    eg = jnp.exp(lg - jnp.max(lg, axis=-1, keepdims=True))
    g_w = pick(eg, g_sel) / jnp.sum(eg, axis=-1, keepdims=True)

    lo = float(EXPERT_LANE0) + float(EXP_PER_GROUP) * g_sel
    is_e = (lane >= lo) & (lane < lo + float(EXP_PER_GROUP))
    eb = jnp.where(is_e, biased, ninf)
    i1 = first_argmax(eb)
    i2 = first_argmax(jnp.where(lane == i1, ninf, eb))
    le = jnp.where(is_e, logits, ninf)
    ee = jnp.exp(le - jnp.max(le, axis=-1, keepdims=True))
    den = jnp.sum(ee, axis=-1, keepdims=True)
    p1 = pick(ee, i1) / den
    p2 = pick(ee, i2) / den
    w1 = g_w * (p1 / (p1 + p2))
    w2 = g_w * (p2 / (p1 + p2))

    hot1 = lane == i1
    hot2 = lane == i2
    cnt = jnp.where(hot1 | hot2, 1.0, 0.0)
    row = lax.broadcasted_iota(I32, (tm, tm), 0)
    col = lax.broadcasted_iota(I32, (tm, tm), 1)
    ltri = jnp.where(col < row, 1.0, 0.0).astype(BF16)
    rank = _dot(ltri, cnt.astype(BF16))
    r1 = jnp.sum(jnp.where(hot1, rank, 0.0), axis=-1, keepdims=True)
    r2 = jnp.sum(jnp.where(hot2, rank, 0.0), axis=-1, keepdims=True)

    n_row = jnp.sum(cnt, axis=0, keepdims=True)
    chunks = jnp.floor((n_row + float(SUBLANES - 1)) * (1.0 / SUBLANES))
    lrow = lax.broadcasted_iota(I32, (ROUTER_LANES, ROUTER_LANES), 0)
    lcol = lax.broadcasted_iota(I32, (ROUTER_LANES, ROUTER_LANES), 1)
    before = jnp.where(lrow < lcol, 1.0, 0.0).astype(BF16)
    chunks8 = jnp.broadcast_to(chunks, (SUBLANES, ROUTER_LANES))
    so_rows = _dot(chunks8.astype(BF16), before) * float(SUBLANES)
    srow = lax.broadcasted_iota(I32, (SUBLANES, ROUTER_LANES), 0)
    tab = jnp.where(srow == 0, jnp.broadcast_to(n_row, (SUBLANES, ROUTER_LANES)), so_rows)

    meta = jnp.where(lane == 0.0, w1, 0.0)
    for j, v in enumerate((w2, i1, i2, r1, r2), start=1):
        meta = jnp.where(lane == float(j), v, meta)

    s1, s2 = _stage_selectors(meta, so_rows[0:1])
    sel = jnp.where(s1 | s2, 1.0, 0.0).astype(BF16)
    return tab, meta, _pack_pairs(_dot_tn(sel, xh))


def _router_kernel(h_ref, g_ref, wr_ref, br_ref, xs_ref, meta_ref, tab_ref):
    xn = _rms(h_ref[...], g_ref[...])
    xh = xn.astype(BF16)
    xl = (xn - xh.astype(F32)).astype(BF16)
    part = _dot(xh, wr_ref[...])
    logits = part[:, :ROUTER_LANES] + part[:, ROUTER_LANES:] + _dot(xl, wr_ref[:, :ROUTER_LANES])
    biased = logits + br_ref[...]
    for s in range(MOE_STEP_TILES):
        rows = slice(s * TM_MOE, (s + 1) * TM_MOE)
        tab, meta, xs = _route_tile(logits[rows], biased[rows], xh[rows])
        tab_ref[s * SUBLANES:(s + 1) * SUBLANES, :] = tab
        meta_ref[rows, :] = meta
        xs_ref[s * STAGE_ROWS:(s + 1) * STAGE_ROWS, :] = xs


def _router(layer, h, g, wr, br):
    t, d = h.shape
    nt = t // TM_MOE
    k = MOE_STEP_TILES
    return pl.pallas_call(
        _router_kernel,
        grid=(nt // k,),
        in_specs=[pl.BlockSpec((k * TM_MOE, d), lambda i: (i, 0))] + [_layer_spec(a, layer) for a in (g, wr, br)],
        out_specs=[
            pl.BlockSpec((k * STAGE_ROWS, d // 2), lambda i: (i, 0)),
            pl.BlockSpec((k * TM_MOE, ROUTER_LANES), lambda i: (i, 0)),
            pl.BlockSpec((k * SUBLANES, ROUTER_LANES), lambda i: (i, 0)),
        ],
        out_shape=[
            jax.ShapeDtypeStruct((nt * STAGE_ROWS, d // 2), I32),
            jax.ShapeDtypeStruct((t, ROUTER_LANES), F32),
            jax.ShapeDtypeStruct((nt * SUBLANES, ROUTER_LANES), F32),
        ],
        compiler_params=pltpu.CompilerParams(dimension_semantics=("arbitrary",), vmem_limit_bytes=VMEM_LIMIT),
        name="router",
    )(h, g, wr, br)


def _match_runs(query, key, n, val, default):
    d = query - key
    inside = (d >= 0.0) & (d < n)
    hit = jnp.sum(jnp.where(inside, 1.0, 0.0), axis=0, keepdims=True)
    got = jnp.sum(jnp.where(inside, val + d, 0.0), axis=0, keepdims=True)
    return jnp.where(hit > 0.0, got, float(default)).astype(I32)


def _chunk_table_kernel(be_ref, tkey_ref, tn_ref, tval_ref, ekey_ref, en_ref, eval_ref, ctab_ref, ytab_ref):
    lanes = ytab_ref.shape[1]
    nruns = tkey_ref.shape[0]
    lane = lax.broadcasted_iota(I32, (1, lanes), 1).astype(F32)
    tiles_met = -(-lanes // STAGE_CHUNKS) + 1
    for k in range(ytab_ref.shape[0]):
        r0 = min((k * lanes) // STAGE_CHUNKS * N_EXPERTS, nruns - tiles_met * N_EXPERTS)
        rs = slice(r0, r0 + tiles_met * N_EXPERTS)
        ytab_ref[k:k + 1, :] = _match_runs(lane + float(k * lanes), tkey_ref[rs, :], tn_ref[rs, :],
                                           tval_ref[rs, :], 0)

    per_block = ctab_ref.shape[1]
    per_expert = nruns // N_EXPERTS
    lane_b = lane[:, :per_block]

    def block(b, carry):
        r0 = pl.multiple_of(be_ref[b] * per_expert, per_expert)
        rs = pl.ds(r0, per_expert)
        query = lane_b + lax.convert_element_type(b * per_block, F32)
        ctab_ref[pl.ds(b, 1), :] = _match_runs(query, ekey_ref[rs, :per_block], en_ref[rs, :per_block],
                                               eval_ref[rs, :per_block], ZERO_CHUNK)
        return carry

    lax.fori_loop(0, ctab_ref.shape[0], block, 0)


def _chunk_tables(block_e, nch, xs_chunk, run_chunk):
    nt = nch.shape[0]
    nb = block_e.shape[0]
    lanes = ROUTER_LANES
    wide = lambda a: jnp.broadcast_to(a.reshape(-1, 1).astype(F32), (a.size, lanes))
    tile_major = [wide(a) for a in (xs_chunk, nch, run_chunk)]
    expert_major = [wide(a.T) for a in (run_chunk, nch, xs_chunk)]
    full = lambda a: pl.BlockSpec(a.shape, lambda i, be: (0, 0))
    n_stage_rows = nt * STAGE_CHUNKS // lanes
    ctab, ytab = pl.pallas_call(
        _chunk_table_kernel,
        grid_spec=pltpu.PrefetchScalarGridSpec(
            num_scalar_prefetch=1,
            grid=(1,),
            in_specs=[full(a) for a in tile_major + expert_major],
            out_specs=[pl.BlockSpec((nb, BM_EXP // SUBLANES), lambda i, be: (0, 0)),
                       pl.BlockSpec((n_stage_rows, lanes), lambda i, be: (0, 0))],
        ),
        out_shape=[jax.ShapeDtypeStruct((nb, BM_EXP // SUBLANES), I32),
                   jax.ShapeDtypeStruct((n_stage_rows, lanes), I32)],
        name="chunks",
    )(block_e, *tile_major, *expert_major)
    return ctab.reshape(-1), ytab.reshape(-1)


def _expert_kernel(be_ref, nu_ref, tab_ref, xs_hbm, wg_ref, wu_ref, wd_ref, y_ref, xbuf, sem):
    b = pl.program_id(0)
    nu = nu_ref[0]
    bm = xbuf.shape[1]
    per_block = bm // SUBLANES

    def start_gather(blk):
        slot = blk % 2
        for c in range(per_block):
            src = pl.multiple_of(tab_ref[blk * per_block + c] * SUBLANES, SUBLANES)
            pltpu.make_async_copy(xs_hbm.at[pl.ds(src, SUBLANES)],
                                  xbuf.at[slot, pl.ds(c * SUBLANES, SUBLANES)], sem.at[slot]).start()

    def wait_gather(blk):
        slot = blk % 2
        pltpu.make_async_copy(xs_hbm.at[pl.ds(0, bm)], xbuf.at[slot], sem.at[slot]).wait()

    @pl.when(b == 0)
    def _():
        start_gather(0)

    @pl.when(b + 1 < nu)
    def _():
        start_gather(b + 1)

    @pl.when(b < nu)
    def _():
        wait_gather(b)
        x = _unpack_pairs(xbuf[b % 2])
        a = _dot(x, wg_ref[...].astype(BF16))
        u = _dot(x, wu_ref[...].astype(BF16))
        hmid = (a * _sigmoid(a)) * u
        y = _dot(hmid.astype(BF16), wd_ref[...].astype(BF16))
        y_ref[...] = _pack_pairs(y.astype(BF16).astype(F32))

    @pl.when(b >= nu)
    def _():
        y_ref[...] = jnp.zeros_like(y_ref)


def _experts(layer, block_e, n_used, tab, xs, wg, wu, wd):
    d, de = wg.shape[-2:]
    dp = xs.shape[1]
    nb = block_e.shape[0]
    wmap = lambda b, be, nu, tb: (layer, be[b], 0, 0)
    return pl.pallas_call(
        _expert_kernel,
        grid_spec=pltpu.PrefetchScalarGridSpec(
            num_scalar_prefetch=3,
            grid=(nb,),
            in_specs=[
                pl.BlockSpec(memory_space=pl.ANY),
                pl.BlockSpec((None, None, d, de), wmap),
                pl.BlockSpec((None, None, d, de), wmap),
                pl.BlockSpec((None, None, de, d), wmap),
            ],
            out_specs=pl.BlockSpec((BM_EXP, dp), lambda b, be, nu, tb: (b, 0)),
            scratch_shapes=[
                pltpu.VMEM((2, BM_EXP, dp), I32),
                pltpu.SemaphoreType.DMA((2,)),
            ],
        ),
        out_shape=jax.ShapeDtypeStruct((nb * BM_EXP, dp), I32),
        compiler_params=pltpu.CompilerParams(dimension_semantics=("arbitrary",), vmem_limit_bytes=VMEM_LIMIT),
        name="experts",
    )(block_e, n_used, tab, xs, wg, wu, wd)


def _combine_kernel(ytab_ref, h_ref, meta_ref, tab_ref, p_ref, wpe_ref, npe_ref, npg_ref,
                    wpg_ref, nf_ref, y_hbm, o_ref, stage, sem, *, final):
    i = pl.program_id(0)

    step_chunks = MOE_STEP_TILES * STAGE_CHUNKS

    def start_gather(step):
        slot = step % 2
        for c in range(step_chunks):
            src = pl.multiple_of(ytab_ref[step * step_chunks + c] * SUBLANES, SUBLANES)
            pltpu.make_async_copy(y_hbm.at[pl.ds(src, SUBLANES)],
                                  stage.at[slot, pl.ds(c * SUBLANES, SUBLANES)], sem.at[slot]).start()

    @pl.when(i == 0)
    def _():
        start_gather(0)

    @pl.when(i + 1 < pl.num_programs(0))
    def _():
        start_gather(i + 1)

    emb = _rms(_dot(p_ref[...].astype(BF16), wpe_ref[...]), npe_ref[...])

    slot = i % 2
    pltpu.make_async_copy(y_hbm.at[pl.ds(0, stage.shape[1])], stage.at[slot], sem.at[slot]).wait()

    moe = []
    for s in range(MOE_STEP_TILES):
        meta = meta_ref[s * TM_MOE:(s + 1) * TM_MOE, :]
        s1, s2 = _stage_selectors(meta, tab_ref[s * SUBLANES + 1:s * SUBLANES + 2, :])
        sel = (jnp.where(s1, meta[:, 0:1], 0.0) + jnp.where(s2, meta[:, 1:2], 0.0)).astype(BF16)
        moe.append(_dot(sel, _unpack_pairs(stage[slot, s * STAGE_ROWS:(s + 1) * STAGE_ROWS, :])))
    h2 = h_ref[...] + jnp.concatenate(moe, axis=0)
    gate = _sigmoid(_dot(_rms(h2, npg_ref[...]).astype(BF16), wpg_ref[...]))
    out = h2 + gate * emb
    if final:
        out = _rms(out, nf_ref[...])
    o_ref[...] = out


def _combine(layer, ytab, h, meta, tab, p, wpe, npe, npg, wpg, nf, y, final):
    t, d = h.shape
    k = MOE_STEP_TILES
    tm = k * TM_MOE
    tile = lambda r, w: pl.BlockSpec((r, w), lambda i, *s: (i, 0))
    return pl.pallas_call(
        functools.partial(_combine_kernel, final=final),
        grid_spec=pltpu.PrefetchScalarGridSpec(
            num_scalar_prefetch=1,
            grid=(t // tm,),
            in_specs=[
                tile(tm, d), tile(tm, ROUTER_LANES), tile(k * SUBLANES, ROUTER_LANES),
                pl.BlockSpec((None, None, tm, p.shape[-1]), lambda i, *s: (layer, 0, i, 0)),
                _layer_spec(wpe, layer), _layer_spec(npe, layer), _layer_spec(npg, layer), _layer_spec(wpg, layer),
                pl.BlockSpec(nf.shape, lambda i, *s: (0, 0)),
                pl.BlockSpec(memory_space=pl.ANY),
            ],
            out_specs=tile(tm, d),
            scratch_shapes=[pltpu.VMEM((2, k * STAGE_ROWS, y.shape[1]), I32), pltpu.SemaphoreType.DMA((2,))],
        ),
        out_shape=jax.ShapeDtypeStruct((t, d), F32),
        compiler_params=pltpu.CompilerParams(dimension_semantics=("arbitrary",), vmem_limit_bytes=VMEM_LIMIT),
        name="combine",
    )(ytab, h, meta, tab, p, wpe, npe, npg, wpg, nf, y)


def _routing_tables(tab, t):
    nt = t // TM_MOE
    tab = tab.reshape(nt, SUBLANES, ROUTER_LANES)[:, :, EXPERT_LANE0:EXPERT_LANE0 + N_EXPERTS]
    counts = tab[:, 0].astype(I32)
    stage_chunk = tab[:, 1].astype(I32) // SUBLANES
    nch = (counts + SUBLANES - 1) // SUBLANES
    per_block = BM_EXP // SUBLANES
    seg = jnp.sum(nch, axis=0)
    padded = (seg + per_block - 1) // per_block * per_block
    pend = jnp.cumsum(padded)
    pstart = pend - padded
    run_chunk = pstart[None, :] + jnp.cumsum(nch, axis=0) - nch
    nb = -(-(TOP_K * t + (SUBLANES - 1) * nt * N_EXPERTS) // BM_EXP) + N_EXPERTS
    bstart = jnp.arange(nb, dtype=I32) * per_block
    block_e = jnp.minimum(jnp.sum((pend[None, :] <= bstart[:, None]).astype(I32), axis=1), N_EXPERTS - 1)
    n_used = pend[-1:] // per_block
    tile_chunk0 = jnp.arange(nt, dtype=I32)[:, None] * STAGE_CHUNKS
    return nch, tile_chunk0 + stage_chunk, run_chunk, block_e.astype(I32), n_used.astype(I32)


def kernel(x, p, norm_mix, w_in, lb_logits, rec_out_gain, sgu_ln_g, sgu_ln_b, sgu_w, sgu_b, w_out, norm_ffn, w_rg, b_rg, w_re, b_re, w_gate, w_up, w_down, norm_pg, w_pg, w_pe, norm_pe, norm_f):
    bsz, seq, d = x.shape
    depth = w_in.shape[0]
    assert bsz == 1 and seq % (MOE_STEP_TILES * TM_MOE) == 0 and seq % TM_MIX == 0
    h = x.reshape(seq, d)

    lb = jnp.cumsum(jax.nn.softmax(lb_logits.astype(F32), axis=0), axis=0)
    lb = lb - lb[0]
    rows = lambda v: v.reshape(depth, 1, -1)
    la, l1, oml = rows(jnp.log(lb)), rows(jnp.log1p(-lb)), rows(1.0 - lb)
    gain = rows(jnp.tile(rec_out_gain, (1, H_REC)))
    sb = jnp.repeat(jnp.swapaxes(sgu_b, 1, 2), D_SGU // H_SGU, axis=2)
    win_b, wout_b, wpg_b, wpe_b = (w.astype(BF16) for w in (w_in, w_out, w_pg, w_pe))
    npad = ROUTER_LANES - N_GROUPS - N_EXPERTS
    wr = jnp.concatenate([w_rg, w_re, jnp.zeros((depth, d, npad), F32)], axis=2)
    wr_hi = wr.astype(BF16)
    wr = jnp.concatenate([wr_hi, (wr - wr_hi.astype(F32)).astype(BF16)], axis=2)
    br = rows(jnp.concatenate([b_rg, b_re, jnp.zeros((depth, npad), F32)], axis=1))
    wsel = jnp.asarray(np.tile(_decay_selectors(), (1, 3)), BF16)
    lev = jnp.asarray(_pair_levels())
    gmix, gffn, gpg, gpe, lng, lnb = (rows(v) for v in (norm_mix, norm_ffn, norm_pg, norm_pe, sgu_ln_g, sgu_ln_b))
    nf = norm_f.reshape(1, d)

    for l in range(depth):
        h = _mixer(l, h, gmix, win_b, la, l1, oml, gain, lng, lnb, sgu_w, sb, wout_b, wsel, lev)
        xs, meta, tab = _router(l, h, gffn, wr, br)
        nch, xs_chunk, run_chunk, block_e, n_used = _routing_tables(tab, seq)
        ctab, ytab = _chunk_tables(block_e, nch, xs_chunk, run_chunk)
        y = _experts(l, block_e, n_used, ctab, xs, w_gate, w_up, w_down)
        h = _combine(l, ytab, h, meta, tab, p, wpe_b, gpe, gpg, wpg_b, nf, y, final=(l == depth - 1))
    return h.reshape(bsz, seq, d)
```

```python
import functools

import numpy as np
import jax
import jax.numpy as jnp
from jax import lax
from jax.experimental import pallas as pl
from jax.experimental.pallas import tpu as pltpu

F32 = jnp.float32
BF16 = jnp.bfloat16
I32 = jnp.int32

EPS = 1e-6
D_REC = 512
D_SGU = 512
H_REC = 4
DK_REC = 128
CHUNK_REC = 64
SGU_CHUNK = 128
H_SGU = 4
N_GROUPS = 4
EXP_PER_GROUP = 8
N_EXPERTS = N_GROUPS * EXP_PER_GROUP
TOP_K = 2
ROUTER_LANES = 128
EXPERT_LANE0 = N_GROUPS
SUBLANES = 8

TM_MIX = 512
TM_MOE = 256
MOE_STEP_TILES = 4
STAGE_ROWS = 768
BM_EXP = 512
VMEM_LIMIT = 48 * 1024 * 1024

assert STAGE_ROWS >= TOP_K * TM_MOE + (SUBLANES - 1) * N_EXPERTS + SUBLANES
STAGE_CHUNKS = STAGE_ROWS // SUBLANES
ZERO_CHUNK = STAGE_CHUNKS - 1
HIGH_HALF = -65536

_LEVELS = (2, 4, 8, 16, 32, 64)


def _decay_selectors():
    t = np.arange(CHUNK_REC)[:, None]
    s = np.arange(CHUNK_REC)[None, :]
    mats = [((t // m == s // m) & (s <= t)) for m in _LEVELS]
    mats += [((t // m == s // m) & (s > t)) for m in _LEVELS]
    return np.concatenate(mats, axis=0).astype(np.float32)


def _pair_levels():
    t = np.arange(CHUNK_REC)[:, None]
    s = np.arange(CHUNK_REC)[None, :]
    x = t ^ s
    lev = np.floor(np.log2(np.maximum(x, 1))).astype(np.int32)
    lev = np.where(t == s, -1, lev)
    lev = np.where(s > t, -2, lev)
    return lev.astype(np.int32)


def _rms(x, g):
    return x * lax.rsqrt(jnp.mean(x * x, axis=-1, keepdims=True) + EPS) * g


def _sigmoid(x):
    return 0.5 * jnp.tanh(0.5 * x) + 0.5


def _gelu_tanh(x):
    return x * (0.5 * (1.0 + jnp.tanh(0.7978845608028654 * (x + 0.044715 * (x * x * x)))))


def _dot(a, b):
    return jnp.dot(a, b, preferred_element_type=F32)


def _dot_nt(a, b):
    return lax.dot_general(a, b, (((1,), (1,)), ((), ())), preferred_element_type=F32)


def _dot_tn(a, b):
    return lax.dot_general(a, b, (((0,), (0,)), ((), ())), preferred_element_type=F32)


def _pack_pairs(x):
    bits = lax.bitcast_convert_type(x, I32)
    half = x.shape[1] // 2
    return lax.shift_right_logical(bits[:, :half], 16) | (bits[:, half:] & HIGH_HALF)


def _unpack_pairs(w):
    lo = lax.bitcast_convert_type(lax.shift_left(w, 16), F32)
    hi = lax.bitcast_convert_type(w & HIGH_HALF, F32)
    return jnp.concatenate([lo, hi], axis=1).astype(BF16)


def _layer_spec(arr, layer):
    zeros = (0,) * (arr.ndim - 1)
    return pl.BlockSpec((None,) + arr.shape[1:], lambda *a: (layer,) + zeros)


def _mixer_kernel(h_ref, gmix_ref, win_ref, la_ref, l1_ref, oml_ref, gain_ref, lng_ref, lnb_ref,
                  ws_ref, sb_ref, wout_ref, wsel_ref, lev_ref, o_ref, st_ref, ocat_ref):
    tm = h_ref.shape[0]

    @pl.when(pl.program_id(0) == 0)
    def _():
        st_ref[...] = jnp.zeros_like(st_ref)

    h = h_ref[...]
    hb = _rms(h, gmix_ref[...]).astype(BF16)

    def proj(j):
        return _dot(hb, win_ref[:, j * 512:(j + 1) * 512])

    zq = proj(0)
    q = zq * _sigmoid(zq)
    fz = proj(1)
    iv = proj(2).astype(BF16)
    zg = proj(3)
    gate = zg * _sigmoid(zg)

    ls = jnp.minimum(fz, 0.0) - jnp.log1p(jnp.exp(-jnp.abs(fz)))
    c = l1_ref[...] + ls
    la = la_ref[...]
    lf = jnp.maximum(la, c) + jnp.log1p(jnp.exp(-jnp.abs(la - c)))
    kk = oml_ref[...] * _sigmoid(-fz)

    wsel = wsel_ref[...]
    lev = lev_ref[...]
    gain = gain_ref[...]
    nlev = len(_LEVELS)
    states = [st_ref[hh] for hh in range(H_REC)]

    zu = proj(4)
    zv = proj(5)
    r128 = lax.broadcasted_iota(I32, (SGU_CHUNK, SGU_CHUNK), 0)
    c128 = lax.broadcasted_iota(I32, (SGU_CHUNK, SGU_CHUNK), 1)
    wtri = [jnp.where(c128 <= r128, ws_ref[hh], 0.0).astype(BF16) for hh in range(H_SGU)]
    sgu_act = {}

    def sgu_activations(c2):
        sl2 = slice(c2 * SGU_CHUNK, (c2 + 1) * SGU_CHUNK)
        ua = _gelu_tanh(zu[sl2])
        va = _gelu_tanh(zv[sl2])
        mu = jnp.mean(va, axis=-1, keepdims=True)
        vc = va - mu
        var = jnp.mean(vc * vc, axis=-1, keepdims=True)
        sgu_act[c2] = (ua, (vc * lax.rsqrt(var + EPS) * lng_ref[...] + lnb_ref[...]).astype(BF16))

    def sgu_mix(c2):
        sl2 = slice(c2 * SGU_CHUNK, (c2 + 1) * SGU_CHUNK)
        ua, vn = sgu_act.pop(c2)
        for hh in range(H_SGU):
            hs = slice(hh * 128, (hh + 1) * 128)
            s = _dot(wtri[hh], vn[:, hs]) + sb_ref[:, hs]
            ocat_ref[sl2, D_REC + hh * 128:D_REC + (hh + 1) * 128] = (ua[:, hs] * s).astype(BF16)

    rec_per_sgu = SGU_CHUNK // CHUNK_REC

    for cidx in range(tm // CHUNK_REC):
        sl = slice(cidx * CHUNK_REC, (cidx + 1) * CHUNK_REC)
        lf_c = lf[sl]
        hi = lf_c.astype(BF16)
        rem = lf_c - hi.astype(F32)
        mid = rem.astype(BF16)
        lo = (rem - mid.astype(F32)).astype(BF16)
        pw = jnp.exp(_dot(wsel, jnp.concatenate([hi, mid, lo], axis=0)))
        f_c = jnp.exp(lf_c)
        for hh in range(H_REC):
            hs = slice(hh * DK_REC, (hh + 1) * DK_REC)
            q_h = q[sl, hs]
            k_h = kk[sl, hs]
            i_h = iv[sl, hs]
            k_b = k_h.astype(BF16)
            sc = jnp.where(lev == -1, _dot_nt(q_h.astype(BF16), k_b), 0.0)
            sc = sc + jnp.where(lev == 0, _dot_nt((q_h * f_c[:, hs]).astype(BF16), k_b), 0.0)
            for j in range(nlev - 1):
                qd = (q_h * pw[j * 64:(j + 1) * 64, hs]).astype(BF16)
                kd = (k_h * pw[(nlev + j) * 64:(nlev + j + 1) * 64, hs]).astype(BF16)
                sc = sc + jnp.where(lev == j + 1, _dot_nt(qd, kd), 0.0)
            jl = nlev - 1
            qb = (q_h * pw[jl * 64:(jl + 1) * 64, hs]).astype(BF16)
            kd = (k_h * pw[(nlev + jl) * 64:(nlev + jl + 1) * 64, hs]).astype(BF16)
            st = states[hh]
            o = _dot(sc.astype(BF16), i_h) + _dot_nt(qb, st.astype(BF16))
            dec = pw[jl * 64 + 63:jl * 64 + 64, hs]
            states[hh] = st * dec + _dot_tn(i_h, kd)
            o = o * lax.rsqrt(jnp.mean(o * o, axis=-1, keepdims=True) + EPS) * gain[:, hs]
            ocat_ref[sl, hs] = (o * gate[sl, hs]).astype(BF16)
        if cidx % rec_per_sgu == 0:
            sgu_activations(cidx // rec_per_sgu)
        if cidx % rec_per_sgu == rec_per_sgu - 1:
            sgu_mix(cidx // rec_per_sgu)

    for hh in range(H_REC):
        st_ref[hh] = states[hh]

    o_ref[...] = h + _dot(ocat_ref[...], wout_ref[...])


def _mixer(layer, h, gmix, win, la, l1, oml, gain, lng, lnb, ws, sb, wout, wsel, lev):
    t, d = h.shape
    tm = min(TM_MIX, t)
    per_layer = (gmix, win, la, l1, oml, gain, lng, lnb, ws, sb, wout)
    return pl.pallas_call(
        _mixer_kernel,
        grid=(t // tm,),
        in_specs=[pl.BlockSpec((tm, d), lambda i: (i, 0))]
        + [_layer_spec(a, layer) for a in per_layer]
        + [pl.BlockSpec(wsel.shape, lambda i: (0, 0)), pl.BlockSpec(lev.shape, lambda i: (0, 0))],
        out_specs=pl.BlockSpec((tm, d), lambda i: (i, 0)),
        out_shape=jax.ShapeDtypeStruct((t, d), F32),
        scratch_shapes=[pltpu.VMEM((H_REC, DK_REC, DK_REC), F32), pltpu.VMEM((tm, d), BF16)],
        compiler_params=pltpu.CompilerParams(dimension_semantics=("arbitrary",), vmem_limit_bytes=VMEM_LIMIT),
        name="mixer",
    )(h, *per_layer, wsel, lev)


def _stage_selectors(meta):
    col = lax.broadcasted_iota(I32, (meta.shape[0], STAGE_ROWS), 1)
    return [col == meta[:, 2 + k:3 + k].astype(I32) for k in range(TOP_K)]


def _route_tile(logits, biased, xh):
    tm = logits.shape[0]
    lane = lax.broadcasted_iota(I32, (tm, ROUTER_LANES), 1).astype(F32)
    ninf = -jnp.inf
    big = float(ROUTER_LANES)

    def first_argmax(v):
        m = jnp.max(v, axis=-1, keepdims=True)
        return jnp.min(jnp.where(v == m, lane, big), axis=-1, keepdims=True)

    def pick(v, idx):
        return jnp.sum(jnp.where(lane == idx, v, 0.0), axis=-1, keepdims=True)

    is_g = lane < float(N_GROUPS)
    g_sel = first_argmax(jnp.where(is_g, biased, ninf))
    lg = jnp.where(is_g, logits, ninf)
    eg = jnp.exp(lg - jnp.max(lg, axis=-1, keepdims=True))
    g_w = pick(eg, g_sel) / jnp.sum(eg, axis=-1, keepdims=True)

    lo = float(EXPERT_LANE0) + float(EXP_PER_GROUP) * g_sel
    is_e = (lane >= lo) & (lane < lo + float(EXP_PER_GROUP))
    eb = jnp.where(is_e, biased, ninf)
    i1 = first_argmax(eb)
    i2 = first_argmax(jnp.where(lane == i1, ninf, eb))
    le = jnp.where(is_e, logits, ninf)
    ee = jnp.exp(le - jnp.max(le, axis=-1, keepdims=True))
    p1 = pick(ee, i1)
    p2 = pick(ee, i2)
    w1 = g_w * (p1 / (p1 + p2))
    w2 = g_w * (p2 / (p1 + p2))

    hot1 = lane == i1
    hot2 = lane == i2
    cnt = jnp.where(hot1 | hot2, 1.0, 0.0)
    row = lax.broadcasted_iota(I32, (tm, tm), 0)
    col = lax.broadcasted_iota(I32, (tm, tm), 1)
    ltri = jnp.where(col < row, 1.0, 0.0).astype(BF16)
    rank = _dot(ltri, cnt.astype(BF16))

    n_row = jnp.sum(cnt, axis=0, keepdims=True)
    chunks = jnp.floor((n_row + float(SUBLANES - 1)) * (1.0 / SUBLANES))
    lrow = lax.broadcasted_iota(I32, (ROUTER_LANES, ROUTER_LANES), 0)
    lcol = lax.broadcasted_iota(I32, (ROUTER_LANES, ROUTER_LANES), 1)
    before = jnp.where(lrow < lcol, 1.0, 0.0).astype(BF16)
    chunks8 = jnp.broadcast_to(chunks, (SUBLANES, ROUTER_LANES))
    so_rows = _dot(chunks8.astype(BF16), before) * float(SUBLANES)
    srow = lax.broadcasted_iota(I32, (SUBLANES, ROUTER_LANES), 0)
    tab = jnp.where(srow == 0, jnp.broadcast_to(n_row, (SUBLANES, ROUTER_LANES)), so_rows)

    place = rank + so_rows[0:1]
    pos1 = jnp.sum(jnp.where(hot1, place, 0.0), axis=-1, keepdims=True)
    pos2 = jnp.sum(jnp.where(hot2, place, 0.0), axis=-1, keepdims=True)
    meta = jnp.where(lane == 0.0, w1, 0.0)
    for j, v in enumerate((w2, pos1, pos2), start=1):
        meta = jnp.where(lane == float(j), v, meta)

    s1, s2 = _stage_selectors(meta)
    sel = jnp.where(s1 | s2, 1.0, 0.0).astype(BF16)
    return tab, meta, _pack_pairs(_dot_tn(sel, xh))


def _router_kernel(h_ref, g_ref, wr_ref, br_ref, xs_ref, meta_ref, tab_ref):
    xn = _rms(h_ref[...], g_ref[...])
    xh = xn.astype(BF16)
    xl = (xn - xh.astype(F32)).astype(BF16)
    part = _dot(xh, wr_ref[...])
    logits = part[:, :ROUTER_LANES] + part[:, ROUTER_LANES:] + _dot(xl, wr_ref[:, :ROUTER_LANES])
    biased = logits + br_ref[...]
    for s in range(MOE_STEP_TILES):
        rows = slice(s * TM_MOE, (s + 1) * TM_MOE)
        tab, meta, xs = _route_tile(logits[rows], biased[rows], xh[rows])
        tab_ref[s * SUBLANES:(s + 1) * SUBLANES, :] = tab
        meta_ref[rows, :] = meta
        xs_ref[s * STAGE_ROWS:(s + 1) * STAGE_ROWS, :] = xs


def _router(layer, h, g, wr, br):
    t, d = h.shape
    nt = t // TM_MOE
    k = MOE_STEP_TILES
    return pl.pallas_call(
        _router_kernel,
        grid=(nt // k,),
        in_specs=[pl.BlockSpec((k * TM_MOE, d), lambda i: (i, 0))] + [_layer_spec(a, layer) for a in (g, wr, br)],
        out_specs=[
            pl.BlockSpec((k * STAGE_ROWS, d // 2), lambda i: (i, 0)),
            pl.BlockSpec((k * TM_MOE, ROUTER_LANES), lambda i: (i, 0)),
            pl.BlockSpec((k * SUBLANES, ROUTER_LANES), lambda i: (i, 0)),
        ],
        out_shape=[
            jax.ShapeDtypeStruct((nt * STAGE_ROWS, d // 2), I32),
            jax.ShapeDtypeStruct((t, ROUTER_LANES), F32),
            jax.ShapeDtypeStruct((nt * SUBLANES, ROUTER_LANES), F32),
        ],
        compiler_params=pltpu.CompilerParams(dimension_semantics=("arbitrary",), vmem_limit_bytes=VMEM_LIMIT),
        name="router",
    )(h, g, wr, br)


def _match_runs(query, key, n, val, default):
    d = query - key
    inside = (d >= 0.0) & (d < n)
    hit = jnp.sum(jnp.where(inside, 1.0, 0.0), axis=0, keepdims=True)
    got = jnp.sum(jnp.where(inside, val + d, 0.0), axis=0, keepdims=True)
    return jnp.where(hit > 0.0, got, float(default)).astype(I32)


def _chunk_table_kernel(be_ref, tkey_ref, tn_ref, tval_ref, ekey_ref, en_ref, eval_ref, ctab_ref, ytab_ref):
    lanes = ytab_ref.shape[1]
    nruns = tkey_ref.shape[0]
    lane = lax.broadcasted_iota(I32, (1, lanes), 1).astype(F32)
    tiles_met = -(-lanes // STAGE_CHUNKS) + 1
    for k in range(ytab_ref.shape[0]):
        r0 = min((k * lanes) // STAGE_CHUNKS * N_EXPERTS, nruns - tiles_met * N_EXPERTS)
        rs = slice(r0, r0 + tiles_met * N_EXPERTS)
        ytab_ref[k:k + 1, :] = _match_runs(lane + float(k * lanes), tkey_ref[rs, :], tn_ref[rs, :],
                                           tval_ref[rs, :], 0)

    per_block = ctab_ref.shape[1]
    per_expert = nruns // N_EXPERTS
    lane_b = lane[:, :per_block]

    def block(b, carry):
        r0 = pl.multiple_of(be_ref[b] * per_expert, per_expert)
        rs = pl.ds(r0, per_expert)
        query = lane_b + lax.convert_element_type(b * per_block, F32)
        ctab_ref[pl.ds(b, 1), :] = _match_runs(query, ekey_ref[rs, :per_block], en_ref[rs, :per_block],
                                               eval_ref[rs, :per_block], ZERO_CHUNK)
        return carry

    lax.fori_loop(0, ctab_ref.shape[0], block, 0)


def _chunk_tables(block_e, nch, xs_chunk, run_chunk):
    nt = nch.shape[0]
    nb = block_e.shape[0]
    lanes = ROUTER_LANES
    wide = lambda a: jnp.broadcast_to(a.reshape(-1, 1).astype(F32), (a.size, lanes))
    tile_major = [wide(a) for a in (xs_chunk, nch, run_chunk)]
    expert_major = [wide(a.T) for a in (run_chunk, nch, xs_chunk)]
    full = lambda a: pl.BlockSpec(a.shape, lambda i, be: (0, 0))
    n_stage_rows = nt * STAGE_CHUNKS // lanes
    ctab, ytab = pl.pallas_call(
        _chunk_table_kernel,
        grid_spec=pltpu.PrefetchScalarGridSpec(
            num_scalar_prefetch=1,
            grid=(1,),
            in_specs=[full(a) for a in tile_major + expert_major],
            out_specs=[pl.BlockSpec((nb, BM_EXP // SUBLANES), lambda i, be: (0, 0)),
                       pl.BlockSpec((n_stage_rows, lanes), lambda i, be: (0, 0))],
        ),
        out_shape=[jax.ShapeDtypeStruct((nb, BM_EXP // SUBLANES), I32),
                   jax.ShapeDtypeStruct((n_stage_rows, lanes), I32)],
        name="chunks",
    )(block_e, *tile_major, *expert_major)
    return ctab.reshape(-1), ytab.reshape(-1)


def _expert_kernel(be_ref, nu_ref, tab_ref, xs_hbm, wg_ref, wu_ref, wd_ref, y_ref, xbuf, sem):
    b = pl.program_id(0)
    nu = nu_ref[0]
    bm = xbuf.shape[1]
    per_block = bm // SUBLANES

    def start_gather(blk):
        slot = blk % 2
        for c in range(per_block):
            src = pl.multiple_of(tab_ref[blk * per_block + c] * SUBLANES, SUBLANES)
            pltpu.make_async_copy(xs_hbm.at[pl.ds(src, SUBLANES)],
                                  xbuf.at[slot, pl.ds(c * SUBLANES, SUBLANES)], sem.at[slot]).start()

    def wait_gather(blk):
        slot = blk % 2
        pltpu.make_async_copy(xs_hbm.at[pl.ds(0, bm)], xbuf.at[slot], sem.at[slot]).wait()

    @pl.when(b == 0)
    def _():
        start_gather(0)

    @pl.when(b + 1 < nu)
    def _():
        start_gather(b + 1)

    @pl.when(b < nu)
    def _():
        wait_gather(b)
        x = _unpack_pairs(xbuf[b % 2])
        a = _dot(x, wg_ref[...].astype(BF16))
        u = _dot(x, wu_ref[...].astype(BF16))
        hmid = (a * _sigmoid(a)) * u
        y = _dot(hmid.astype(BF16), wd_ref[...].astype(BF16))
        y_ref[...] = _pack_pairs(y.astype(BF16).astype(F32))

    @pl.when(b >= nu)
    def _():
        y_ref[...] = jnp.zeros_like(y_ref)


def _experts(layer, block_e, n_used, tab, xs, wg, wu, wd):
    d, de = wg.shape[-2:]
    dp = xs.shape[1]
    nb = block_e.shape[0]
    wmap = lambda b, be, nu, tb: (layer, be[b], 0, 0)
    return pl.pallas_call(
        _expert_kernel,
        grid_spec=pltpu.PrefetchScalarGridSpec(
            num_scalar_prefetch=3,
            grid=(nb,),
            in_specs=[
                pl.BlockSpec(memory_space=pl.ANY),
                pl.BlockSpec((None, None, d, de), wmap),
                pl.BlockSpec((None, None, d, de), wmap),
                pl.BlockSpec((None, None, de, d), wmap),
            ],
            out_specs=pl.BlockSpec((BM_EXP, dp), lambda b, be, nu, tb: (b, 0)),
            scratch_shapes=[
                pltpu.VMEM((2, BM_EXP, dp), I32),
                pltpu.SemaphoreType.DMA((2,)),
            ],
        ),
        out_shape=jax.ShapeDtypeStruct((nb * BM_EXP, dp), I32),
        compiler_params=pltpu.CompilerParams(dimension_semantics=("arbitrary",), vmem_limit_bytes=VMEM_LIMIT),
        name="experts",
    )(block_e, n_used, tab, xs, wg, wu, wd)


def _combine_kernel(ytab_ref, h_ref, meta_ref, p_ref, wpe_ref, npe_ref, npg_ref,
                    wpg_ref, nf_ref, y_hbm, o_ref, stage, sem, *, final):
    i = pl.program_id(0)

    step_chunks = MOE_STEP_TILES * STAGE_CHUNKS

    def start_gather(step):
        slot = step % 2
        for c in range(step_chunks):
            src = pl.multiple_of(ytab_ref[step * step_chunks + c] * SUBLANES, SUBLANES)
            pltpu.make_async_copy(y_hbm.at[pl.ds(src, SUBLANES)],
                                  stage.at[slot, pl.ds(c * SUBLANES, SUBLANES)], sem.at[slot]).start()

    @pl.when(i == 0)
    def _():
        start_gather(0)

    @pl.when(i + 1 < pl.num_programs(0))
    def _():
        start_gather(i + 1)

    emb = _rms(_dot(p_ref[...].astype(BF16), wpe_ref[...]), npe_ref[...])

    slot = i % 2
    pltpu.make_async_copy(y_hbm.at[pl.ds(0, stage.shape[1])], stage.at[slot], sem.at[slot]).wait()

    moe = []
    for s in range(MOE_STEP_TILES):
        meta = meta_ref[s * TM_MOE:(s + 1) * TM_MOE, :]
        s1, s2 = _stage_selectors(meta)
        sel = (jnp.where(s1, meta[:, 0:1], 0.0) + jnp.where(s2, meta[:, 1:2], 0.0)).astype(BF16)
        moe.append(_dot(sel, _unpack_pairs(stage[slot, s * STAGE_ROWS:(s + 1) * STAGE_ROWS, :])))
    h2 = h_ref[...] + jnp.concatenate(moe, axis=0)
    gate = _sigmoid(_dot(_rms(h2, npg_ref[...]).astype(BF16), wpg_ref[...]))
    out = h2 + gate * emb
    if final:
        out = _rms(out, nf_ref[...])
    o_ref[...] = out


def _combine(layer, ytab, h, meta, p, wpe, npe, npg, wpg, nf, y, final):
    t, d = h.shape
    k = MOE_STEP_TILES
    tm = k * TM_MOE
    tile = lambda r, w: pl.BlockSpec((r, w), lambda i, *s: (i, 0))
    return pl.pallas_call(
        functools.partial(_combine_kernel, final=final),
        grid_spec=pltpu.PrefetchScalarGridSpec(
            num_scalar_prefetch=1,
            grid=(t // tm,),
            in_specs=[
                tile(tm, d), tile(tm, ROUTER_LANES),
                pl.BlockSpec((None, None, tm, p.shape[-1]), lambda i, *s: (layer, 0, i, 0)),
                _layer_spec(wpe, layer), _layer_spec(npe, layer), _layer_spec(npg, layer), _layer_spec(wpg, layer),
                pl.BlockSpec(nf.shape, lambda i, *s: (0, 0)),
                pl.BlockSpec(memory_space=pl.ANY),
            ],
            out_specs=tile(tm, d),
            scratch_shapes=[pltpu.VMEM((2, k * STAGE_ROWS, y.shape[1]), I32), pltpu.SemaphoreType.DMA((2,))],
        ),
        out_shape=jax.ShapeDtypeStruct((t, d), F32),
        compiler_params=pltpu.CompilerParams(dimension_semantics=("arbitrary",), vmem_limit_bytes=VMEM_LIMIT),
        name="combine",
    )(ytab, h, meta, p, wpe, npe, npg, wpg, nf, y)


def _routing_tables(tab, t):
    nt = t // TM_MOE
    tab = tab.reshape(nt, SUBLANES, ROUTER_LANES)[:, :, EXPERT_LANE0:EXPERT_LANE0 + N_EXPERTS]
    counts = tab[:, 0].astype(I32)
    stage_chunk = tab[:, 1].astype(I32) // SUBLANES
    nch = (counts + SUBLANES - 1) // SUBLANES
    per_block = BM_EXP // SUBLANES
    seg = jnp.sum(nch, axis=0)
    padded = (seg + per_block - 1) // per_block * per_block
    pend = jnp.cumsum(padded)
    pstart = pend - padded
    run_chunk = pstart[None, :] + jnp.cumsum(nch, axis=0) - nch
    nb = -(-(TOP_K * t + (SUBLANES - 1) * nt * N_EXPERTS) // BM_EXP) + N_EXPERTS
    bstart = jnp.arange(nb, dtype=I32) * per_block
    block_e = jnp.minimum(jnp.sum((pend[None, :] <= bstart[:, None]).astype(I32), axis=1), N_EXPERTS - 1)
    n_used = pend[-1:] // per_block
    tile_chunk0 = jnp.arange(nt, dtype=I32)[:, None] * STAGE_CHUNKS
    return nch, tile_chunk0 + stage_chunk, run_chunk, block_e.astype(I32), n_used.astype(I32)


def kernel(x, p, norm_mix, w_in, lb_logits, rec_out_gain, sgu_ln_g, sgu_ln_b, sgu_w, sgu_b, w_out, norm_ffn, w_rg, b_rg, w_re, b_re, w_gate, w_up, w_down, norm_pg, w_pg, w_pe, norm_pe, norm_f):
    bsz, seq, d = x.shape
    depth = w_in.shape[0]
    assert bsz == 1 and seq % (MOE_STEP_TILES * TM_MOE) == 0 and seq % TM_MIX == 0
    h = x.reshape(seq, d)

    lb = jnp.cumsum(jax.nn.softmax(lb_logits.astype(F32), axis=0), axis=0)
    lb = lb - lb[0]
    rows = lambda v: v.reshape(depth, 1, -1)
    la, l1, oml = rows(jnp.log(lb)), rows(jnp.log1p(-lb)), rows(1.0 - lb)
    gain = rows(jnp.tile(rec_out_gain, (1, H_REC)))
    sb = jnp.repeat(jnp.swapaxes(sgu_b, 1, 2), D_SGU // H_SGU, axis=2)
    win_b, wout_b, wpg_b, wpe_b = (w.astype(BF16) for w in (w_in, w_out, w_pg, w_pe))
    npad = ROUTER_LANES - N_GROUPS - N_EXPERTS
    wr = jnp.concatenate([w_rg, w_re, jnp.zeros((depth, d, npad), F32)], axis=2)
    wr_hi = wr.astype(BF16)
    wr = jnp.concatenate([wr_hi, (wr - wr_hi.astype(F32)).astype(BF16)], axis=2)
    br = rows(jnp.concatenate([b_rg, b_re, jnp.zeros((depth, npad), F32)], axis=1))
    wsel = jnp.asarray(np.tile(_decay_selectors(), (1, 3)), BF16)
    lev = jnp.asarray(_pair_levels())
    gmix, gffn, gpg, gpe, lng, lnb = (rows(v) for v in (norm_mix, norm_ffn, norm_pg, norm_pe, sgu_ln_g, sgu_ln_b))
    nf = norm_f.reshape(1, d)

    for l in range(depth):
        h = _mixer(l, h, gmix, win_b, la, l1, oml, gain, lng, lnb, sgu_w, sb, wout_b, wsel, lev)
        xs, meta, tab = _router(l, h, gffn, wr, br)
        nch, xs_chunk, run_chunk, block_e, n_used = _routing_tables(tab, seq)
        ctab, ytab = _chunk_tables(block_e, nch, xs_chunk, run_chunk)
        y = _experts(l, block_e, n_used, ctab, xs, w_gate, w_up, w_down)
        h = _combine(l, ytab, h, meta, p, wpe_b, gpe, gpg, wpg_b, nf, y, final=(l == depth - 1))
    return h.reshape(bsz, seq, d)
```

```python
import functools

import numpy as np
import jax
import jax.numpy as jnp
from jax import lax
from jax.experimental import pallas as pl
from jax.experimental.pallas import tpu as pltpu

F32 = jnp.float32
BF16 = jnp.bfloat16
I32 = jnp.int32

EPS = 1e-6
D_REC = 512
D_SGU = 512
H_REC = 4
DK_REC = 128
CHUNK_REC = 64
SGU_CHUNK = 128
H_SGU = 4
N_GROUPS = 4
EXP_PER_GROUP = 8
N_EXPERTS = N_GROUPS * EXP_PER_GROUP
TOP_K = 2
ROUTER_LANES = 128
EXPERT_LANE0 = N_GROUPS
SUBLANES = 8

TM_MIX = 512
TM_MOE = 256
MOE_STEP_TILES = 4
STAGE_ROWS = 768
BM_EXP = 512
VMEM_LIMIT = 48 * 1024 * 1024

assert STAGE_ROWS >= TOP_K * TM_MOE + (SUBLANES - 1) * N_EXPERTS + SUBLANES
STAGE_CHUNKS = STAGE_ROWS // SUBLANES
ZERO_CHUNK = STAGE_CHUNKS - 1
HIGH_HALF = -65536

_LEVELS = (2, 4, 8, 16, 32, 64)


def _decay_selectors():
    t = np.arange(CHUNK_REC)[:, None]
    s = np.arange(CHUNK_REC)[None, :]
    mats = [((t // m == s // m) & (s <= t)) for m in _LEVELS]
    mats += [((t // m == s // m) & (s > t)) for m in _LEVELS]
    return np.concatenate(mats, axis=0).astype(np.float32)


def _pair_levels():
    t = np.arange(CHUNK_REC)[:, None]
    s = np.arange(CHUNK_REC)[None, :]
    x = t ^ s
    lev = np.floor(np.log2(np.maximum(x, 1))).astype(np.int32)
    lev = np.where(t == s, -1, lev)
    lev = np.where(s > t, -2, lev)
    return lev.astype(np.int32)


def _rms(x, g):
    return x * lax.rsqrt(jnp.mean(x * x, axis=-1, keepdims=True) + EPS) * g


def _sigmoid(x):
    return 0.5 * jnp.tanh(0.5 * x) + 0.5


def _gelu_tanh(x):
    return x * (0.5 * (1.0 + jnp.tanh(0.7978845608028654 * (x + 0.044715 * (x * x * x)))))


def _dot(a, b):
    return jnp.dot(a, b, preferred_element_type=F32)


def _dot_nt(a, b):
    return lax.dot_general(a, b, (((1,), (1,)), ((), ())), preferred_element_type=F32)


def _dot_tn(a, b):
    return lax.dot_general(a, b, (((0,), (0,)), ((), ())), preferred_element_type=F32)


def _pack_pairs(x):
    bits = lax.bitcast_convert_type(x, I32)
    half = x.shape[1] // 2
    return lax.shift_right_logical(bits[:, :half], 16) | (bits[:, half:] & HIGH_HALF)


def _unpack_pairs(w):
    lo = lax.bitcast_convert_type(lax.shift_left(w, 16), F32)
    hi = lax.bitcast_convert_type(w & HIGH_HALF, F32)
    return jnp.concatenate([lo, hi], axis=1).astype(BF16)


def _layer_spec(arr, layer):
    zeros = (0,) * (arr.ndim - 1)
    return pl.BlockSpec((None,) + arr.shape[1:], lambda *a: (layer,) + zeros)


def _mixer_kernel(h_ref, gmix_ref, win_ref, la_ref, l1_ref, oml_ref, gain_ref, lng_ref, lnb_ref,
                  ws_ref, sb_ref, wout_ref, wsel_ref, lev_ref, o_ref, st_ref, ocat_ref):
    tm = h_ref.shape[0]

    @pl.when(pl.program_id(0) == 0)
    def _():
        st_ref[...] = jnp.zeros_like(st_ref)

    h = h_ref[...]
    hb = _rms(h, gmix_ref[...]).astype(BF16)

    def proj(j):
        return _dot(hb, win_ref[:, j * 512:(j + 1) * 512])

    zq = proj(0)
    q = zq * _sigmoid(zq)
    fz = proj(1)
    iv = proj(2).astype(BF16)
    zg = proj(3)
    gate = zg * _sigmoid(zg)

    ls = jnp.minimum(fz, 0.0) - jnp.log1p(jnp.exp(-jnp.abs(fz)))
    c = l1_ref[...] + ls
    la = la_ref[...]
    lf = jnp.maximum(la, c) + jnp.log1p(jnp.exp(-jnp.abs(la - c)))
    kk = oml_ref[...] * _sigmoid(-fz)

    wsel = wsel_ref[...]
    lev = lev_ref[...]
    gain = gain_ref[...]
    nlev = len(_LEVELS)
    states = [st_ref[hh] for hh in range(H_REC)]

    zu = proj(4)
    zv = proj(5)
    r128 = lax.broadcasted_iota(I32, (SGU_CHUNK, SGU_CHUNK), 0)
    c128 = lax.broadcasted_iota(I32, (SGU_CHUNK, SGU_CHUNK), 1)
    wtri = [jnp.where(c128 <= r128, ws_ref[hh], 0.0).astype(BF16) for hh in range(H_SGU)]
    sgu_act = {}

    def sgu_activations(c2):
        sl2 = slice(c2 * SGU_CHUNK, (c2 + 1) * SGU_CHUNK)
        ua = _gelu_tanh(zu[sl2])
        va = _gelu_tanh(zv[sl2])
        mu = jnp.mean(va, axis=-1, keepdims=True)
        vc = va - mu
        var = jnp.mean(vc * vc, axis=-1, keepdims=True)
        sgu_act[c2] = (ua, (vc * lax.rsqrt(var + EPS) * lng_ref[...] + lnb_ref[...]).astype(BF16))

    def sgu_mix(c2):
        sl2 = slice(c2 * SGU_CHUNK, (c2 + 1) * SGU_CHUNK)
        ua, vn = sgu_act.pop(c2)
        for hh in range(H_SGU):
            hs = slice(hh * 128, (hh + 1) * 128)
            s = _dot(wtri[hh], vn[:, hs]) + sb_ref[:, hs]
            ocat_ref[sl2, D_REC + hh * 128:D_REC + (hh + 1) * 128] = (ua[:, hs] * s).astype(BF16)

    rec_per_sgu = SGU_CHUNK // CHUNK_REC

    for cidx in range(tm // CHUNK_REC):
        sl = slice(cidx * CHUNK_REC, (cidx + 1) * CHUNK_REC)
        lf_c = lf[sl]
        hi = lf_c.astype(BF16)
        rem = lf_c - hi.astype(F32)
        mid = rem.astype(BF16)
        lo = (rem - mid.astype(F32)).astype(BF16)
        pw = jnp.exp(_dot(wsel, jnp.concatenate([hi, mid, lo], axis=0)))
        f_c = jnp.exp(lf_c)
        for hh in range(H_REC):
            hs = slice(hh * DK_REC, (hh + 1) * DK_REC)
            q_h = q[sl, hs]
            k_h = kk[sl, hs]
            i_h = iv[sl, hs]
            k_b = k_h.astype(BF16)
            sc = jnp.where(lev == -1, _dot_nt(q_h.astype(BF16), k_b), 0.0)
            sc = sc + jnp.where(lev == 0, _dot_nt((q_h * f_c[:, hs]).astype(BF16), k_b), 0.0)
            for j in range(nlev - 1):
                qd = (q_h * pw[j * 64:(j + 1) * 64, hs]).astype(BF16)
                kd = (k_h * pw[(nlev + j) * 64:(nlev + j + 1) * 64, hs]).astype(BF16)
                sc = sc + jnp.where(lev == j + 1, _dot_nt(qd, kd), 0.0)
            jl = nlev - 1
            qb = (q_h * pw[jl * 64:(jl + 1) * 64, hs]).astype(BF16)
            kd = (k_h * pw[(nlev + jl) * 64:(nlev + jl + 1) * 64, hs]).astype(BF16)
            st = states[hh]
            o = _dot(sc.astype(BF16), i_h) + _dot_nt(qb, st.astype(BF16))
            dec = pw[jl * 64 + 63:jl * 64 + 64, hs]
            states[hh] = st * dec + _dot_tn(i_h, kd)
            o = o * lax.rsqrt(jnp.mean(o * o, axis=-1, keepdims=True) + EPS) * gain[:, hs]
            ocat_ref[sl, hs] = (o * gate[sl, hs]).astype(BF16)
        if cidx % rec_per_sgu == 0:
            sgu_activations(cidx // rec_per_sgu)
        if cidx % rec_per_sgu == rec_per_sgu - 1:
            sgu_mix(cidx // rec_per_sgu)

    for hh in range(H_REC):
        st_ref[hh] = states[hh]

    o_ref[...] = h + _dot(ocat_ref[...], wout_ref[...])


def _mixer(layer, h, gmix, win, la, l1, oml, gain, lng, lnb, ws, sb, wout, wsel, lev):
    t, d = h.shape
    tm = min(TM_MIX, t)
    per_layer = (gmix, win, la, l1, oml, gain, lng, lnb, ws, sb, wout)
    return pl.pallas_call(
        _mixer_kernel,
        grid=(t // tm,),
        in_specs=[pl.BlockSpec((tm, d), lambda i: (i, 0))]
        + [_layer_spec(a, layer) for a in per_layer]
        + [pl.BlockSpec(wsel.shape, lambda i: (0, 0)), pl.BlockSpec(lev.shape, lambda i: (0, 0))],
        out_specs=pl.BlockSpec((tm, d), lambda i: (i, 0)),
        out_shape=jax.ShapeDtypeStruct((t, d), F32),
        scratch_shapes=[pltpu.VMEM((H_REC, DK_REC, DK_REC), F32), pltpu.VMEM((tm, d), BF16)],
        compiler_params=pltpu.CompilerParams(dimension_semantics=("arbitrary",), vmem_limit_bytes=VMEM_LIMIT),
        name="mixer",
    )(h, *per_layer, wsel, lev)


def _stage_selectors(meta):
    col = lax.broadcasted_iota(I32, (meta.shape[0], STAGE_ROWS), 1)
    return [col == meta[:, 2 + k:3 + k].astype(I32) for k in range(TOP_K)]


def _route_tile(logits, biased, xh):
    tm = logits.shape[0]
    lane = lax.broadcasted_iota(I32, (tm, ROUTER_LANES), 1).astype(F32)
    ninf = -jnp.inf
    big = float(ROUTER_LANES)

    def first_argmax(v):
        m = jnp.max(v, axis=-1, keepdims=True)
        return jnp.min(jnp.where(v == m, lane, big), axis=-1, keepdims=True)

    def pick(v, idx):
        return jnp.sum(jnp.where(lane == idx, v, 0.0), axis=-1, keepdims=True)

    is_g = lane < float(N_GROUPS)
    g_sel = first_argmax(jnp.where(is_g, biased, ninf))
    lg = jnp.where(is_g, logits, ninf)
    eg = jnp.exp(lg - jnp.max(lg, axis=-1, keepdims=True))
    g_w = pick(eg, g_sel) / jnp.sum(eg, axis=-1, keepdims=True)

    lo = float(EXPERT_LANE0) + float(EXP_PER_GROUP) * g_sel
    is_e = (lane >= lo) & (lane < lo + float(EXP_PER_GROUP))
    eb = jnp.where(is_e, biased, ninf)
    i1 = first_argmax(eb)
    i2 = first_argmax(jnp.where(lane == i1, ninf, eb))
    le = jnp.where(is_e, logits, ninf)
    ee = jnp.exp(le - jnp.max(le, axis=-1, keepdims=True))
    p1 = pick(ee, i1)
    p2 = pick(ee, i2)
    w1 = g_w * (p1 / (p1 + p2))
    w2 = g_w * (p2 / (p1 + p2))

    hot1 = lane == i1
    hot2 = lane == i2
    cnt = jnp.where(hot1 | hot2, 1.0, 0.0)
    row = lax.broadcasted_iota(I32, (tm, tm), 0)
    col = lax.broadcasted_iota(I32, (tm, tm), 1)
    ltri = jnp.where(col < row, 1.0, 0.0).astype(BF16)
    rank = _dot(ltri, cnt.astype(BF16))

    n_row = jnp.sum(cnt, axis=0, keepdims=True)
    chunks = jnp.floor((n_row + float(SUBLANES - 1)) * (1.0 / SUBLANES))
    lrow = lax.broadcasted_iota(I32, (ROUTER_LANES, ROUTER_LANES), 0)
    lcol = lax.broadcasted_iota(I32, (ROUTER_LANES, ROUTER_LANES), 1)
    before = jnp.where(lrow < lcol, 1.0, 0.0).astype(BF16)
    chunks8 = jnp.broadcast_to(chunks, (SUBLANES, ROUTER_LANES))
    so_rows = _dot(chunks8.astype(BF16), before) * float(SUBLANES)
    srow = lax.broadcasted_iota(I32, (SUBLANES, ROUTER_LANES), 0)
    tab = jnp.where(srow == 0, jnp.broadcast_to(n_row, (SUBLANES, ROUTER_LANES)), so_rows)

    place = rank + so_rows[0:1]
    pos1 = jnp.sum(jnp.where(hot1, place, 0.0), axis=-1, keepdims=True)
    pos2 = jnp.sum(jnp.where(hot2, place, 0.0), axis=-1, keepdims=True)
    meta = jnp.where(lane == 0.0, w1, 0.0)
    for j, v in enumerate((w2, pos1, pos2), start=1):
        meta = jnp.where(lane == float(j), v, meta)

    s1, s2 = _stage_selectors(meta)
    sel = jnp.where(s1 | s2, 1.0, 0.0).astype(BF16)
    return tab, meta, _pack_pairs(_dot_tn(sel, xh))


def _router_kernel(h_ref, g_ref, wr_ref, br_ref, xs_ref, meta_ref, tab_ref):
    xn = _rms(h_ref[...], g_ref[...])
    xh = xn.astype(BF16)
    xl = (xn - xh.astype(F32)).astype(BF16)
    part = _dot(xh, wr_ref[...])
    logits = part[:, :ROUTER_LANES] + part[:, ROUTER_LANES:] + _dot(xl, wr_ref[:, :ROUTER_LANES])
    biased = logits + br_ref[...]
    for s in range(MOE_STEP_TILES):
        rows = slice(s * TM_MOE, (s + 1) * TM_MOE)
        tab, meta, xs = _route_tile(logits[rows], biased[rows], xh[rows])
        tab_ref[s * SUBLANES:(s + 1) * SUBLANES, :] = tab
        meta_ref[rows, :] = meta
        xs_ref[s * STAGE_ROWS:(s + 1) * STAGE_ROWS, :] = xs


def _router(layer, h, g, wr, br):
    t, d = h.shape
    nt = t // TM_MOE
    k = MOE_STEP_TILES
    return pl.pallas_call(
        _router_kernel,
        grid=(nt // k,),
        in_specs=[pl.BlockSpec((k * TM_MOE, d), lambda i: (i, 0))] + [_layer_spec(a, layer) for a in (g, wr, br)],
        out_specs=[
            pl.BlockSpec((k * STAGE_ROWS, d // 2), lambda i: (i, 0)),
            pl.BlockSpec((k * TM_MOE, ROUTER_LANES), lambda i: (i, 0)),
            pl.BlockSpec((k * SUBLANES, ROUTER_LANES), lambda i: (i, 0)),
        ],
        out_shape=[
            jax.ShapeDtypeStruct((nt * STAGE_ROWS, d // 2), I32),
            jax.ShapeDtypeStruct((t, ROUTER_LANES), F32),
            jax.ShapeDtypeStruct((nt * SUBLANES, ROUTER_LANES), F32),
        ],
        compiler_params=pltpu.CompilerParams(dimension_semantics=("arbitrary",), vmem_limit_bytes=VMEM_LIMIT),
        name="router",
    )(h, g, wr, br)


def _match_runs(query, key, n, val, default):
    d = query - key
    inside = (d >= 0.0) & (d < n)
    hit = jnp.sum(jnp.where(inside, 1.0, 0.0), axis=0, keepdims=True)
    got = jnp.sum(jnp.where(inside, val + d, 0.0), axis=0, keepdims=True)
    return jnp.where(hit > 0.0, got, float(default)).astype(I32)


def _chunk_table_kernel(be_ref, tkey_ref, tn_ref, tval_ref, ekey_ref, en_ref, eval_ref, ctab_ref, ytab_ref):
    lanes = ytab_ref.shape[1]
    nruns = tkey_ref.shape[0]
    lane = lax.broadcasted_iota(I32, (1, lanes), 1).astype(F32)
    tiles_met = -(-lanes // STAGE_CHUNKS) + 1
    for k in range(ytab_ref.shape[0]):
        r0 = min((k * lanes) // STAGE_CHUNKS * N_EXPERTS, nruns - tiles_met * N_EXPERTS)
        rs = slice(r0, r0 + tiles_met * N_EXPERTS)
        ytab_ref[k:k + 1, :] = _match_runs(lane + float(k * lanes), tkey_ref[rs, :], tn_ref[rs, :],
                                           tval_ref[rs, :], 0)

    per_block = ctab_ref.shape[1]
    per_expert = nruns // N_EXPERTS
    lane_b = lane[:, :per_block]

    def block(b, carry):
        r0 = pl.multiple_of(be_ref[b] * per_expert, per_expert)
        rs = pl.ds(r0, per_expert)
        query = lane_b + lax.convert_element_type(b * per_block, F32)
        ctab_ref[pl.ds(b, 1), :] = _match_runs(query, ekey_ref[rs, :per_block], en_ref[rs, :per_block],
                                               eval_ref[rs, :per_block], ZERO_CHUNK)
        return carry

    lax.fori_loop(0, ctab_ref.shape[0], block, 0)


def _chunk_tables(block_e, nch, xs_chunk, run_chunk):
    nt = nch.shape[0]
    nb = block_e.shape[0]
    lanes = ROUTER_LANES
    wide = lambda a: jnp.broadcast_to(a.reshape(-1, 1).astype(F32), (a.size, lanes))
    tile_major = [wide(a) for a in (xs_chunk, nch, run_chunk)]
    expert_major = [wide(a.T) for a in (run_chunk, nch, xs_chunk)]
    full = lambda a: pl.BlockSpec(a.shape, lambda i, be: (0, 0))
    n_stage_rows = nt * STAGE_CHUNKS // lanes
    ctab, ytab = pl.pallas_call(
        _chunk_table_kernel,
        grid_spec=pltpu.PrefetchScalarGridSpec(
            num_scalar_prefetch=1,
            grid=(1,),
            in_specs=[full(a) for a in tile_major + expert_major],
            out_specs=[pl.BlockSpec((nb, BM_EXP // SUBLANES), lambda i, be: (0, 0)),
                       pl.BlockSpec((n_stage_rows, lanes), lambda i, be: (0, 0))],
        ),
        out_shape=[jax.ShapeDtypeStruct((nb, BM_EXP // SUBLANES), I32),
                   jax.ShapeDtypeStruct((n_stage_rows, lanes), I32)],
        name="chunks",
    )(block_e, *tile_major, *expert_major)
    return ctab.reshape(-1), ytab.reshape(-1)


def _expert_kernel(be_ref, nu_ref, plan_ref, tab_ref, xs_hbm, wg_hbm, wu_hbm, wd_hbm, y_ref,
                   xbuf, wgbuf, wubuf, wdbuf, sem, wsem, *, layer):
    b = pl.program_id(0)
    nu = nu_ref[0]
    bm = xbuf.shape[1]
    per_block = bm // SUBLANES

    def start_gather(blk):
        slot = blk % 2
        for c in range(per_block):
            src = pl.multiple_of(tab_ref[blk * per_block + c] * SUBLANES, SUBLANES)
            pltpu.make_async_copy(xs_hbm.at[pl.ds(src, SUBLANES)],
                                  xbuf.at[slot, pl.ds(c * SUBLANES, SUBLANES)], sem.at[slot]).start()

    def wait_gather(blk):
        slot = blk % 2
        pltpu.make_async_copy(xs_hbm.at[pl.ds(0, bm)], xbuf.at[slot], sem.at[slot]).wait()

    def weight_copies(e, slot):
        return [pltpu.make_async_copy(w_hbm.at[layer, e], buf.at[slot], wsem.at[slot])
                for w_hbm, buf in ((wg_hbm, wgbuf), (wu_hbm, wubuf), (wd_hbm, wdbuf))]

    @pl.when(b == 0)
    def _():
        start_gather(0)
        for cp in weight_copies(be_ref[0], plan_ref[0]):
            cp.start()

    @pl.when(b + 1 < nu)
    def _():
        start_gather(b + 1)

    @pl.when(b < nu)
    def _():
        wslot = plan_ref[3 * b]

        @pl.when(plan_ref[3 * b + 1] == 1)
        def _():
            for cp in weight_copies(be_ref[b], wslot):
                cp.wait()
            nxt = plan_ref[3 * b + 2]

            @pl.when(nxt >= 0)
            def _():
                for cp in weight_copies(nxt, 1 - wslot):
                    cp.start()

        wait_gather(b)
        x = _unpack_pairs(xbuf[b % 2])
        a = _dot(x, wgbuf[wslot].astype(BF16))
        u = _dot(x, wubuf[wslot].astype(BF16))
        hmid = (a * _sigmoid(a)) * u
        y = _dot(hmid.astype(BF16), wdbuf[wslot].astype(BF16))
        y_ref[...] = _pack_pairs(y.astype(BF16).astype(F32))

    @pl.when(b >= nu)
    def _():
        y_ref[...] = jnp.zeros_like(y_ref)


def _weight_plan(block_e, n_used):
    nb = block_e.shape[0]
    idx = jnp.arange(nb, dtype=I32)
    used = idx < n_used[0]
    first = used & jnp.concatenate([jnp.ones((1,), bool), block_e[1:] != block_e[:-1]])
    slot = (jnp.cumsum(first.astype(I32)) - 1) % 2
    nxt_idx = lax.cummin(jnp.where(first, idx, nb)[::-1])[::-1]
    nxt_idx = jnp.concatenate([nxt_idx[1:], jnp.full((1,), nb, I32)])
    nxt = jnp.where(nxt_idx < nb, block_e[jnp.minimum(nxt_idx, nb - 1)], -1)
    return jnp.stack([slot, first.astype(I32), nxt], axis=1).reshape(-1).astype(I32)


def _experts(layer, block_e, n_used, tab, xs, wg, wu, wd):
    d, de = wg.shape[-2:]
    dp = xs.shape[1]
    nb = block_e.shape[0]
    hbm = pl.BlockSpec(memory_space=pl.ANY)
    return pl.pallas_call(
        functools.partial(_expert_kernel, layer=layer),
        grid_spec=pltpu.PrefetchScalarGridSpec(
            num_scalar_prefetch=4,
            grid=(nb,),
            in_specs=[hbm, hbm, hbm, hbm],
            out_specs=pl.BlockSpec((BM_EXP, dp), lambda b, *s: (b, 0)),
            scratch_shapes=[
                pltpu.VMEM((2, BM_EXP, dp), I32),
                pltpu.VMEM((2, d, de), F32), pltpu.VMEM((2, d, de), F32), pltpu.VMEM((2, de, d), F32),
                pltpu.SemaphoreType.DMA((2,)), pltpu.SemaphoreType.DMA((2,)),
            ],
        ),
        out_shape=jax.ShapeDtypeStruct((nb * BM_EXP, dp), I32),
        compiler_params=pltpu.CompilerParams(dimension_semantics=("arbitrary",), vmem_limit_bytes=VMEM_LIMIT),
        name="experts",
    )(block_e, n_used, _weight_plan(block_e, n_used), tab, xs, wg, wu, wd)


def _combine_kernel(ytab_ref, h_ref, meta_ref, p_ref, wpe_ref, npe_ref, npg_ref,
                    wpg_ref, nf_ref, y_hbm, o_ref, stage, sem, *, final):
    i = pl.program_id(0)

    step_chunks = MOE_STEP_TILES * STAGE_CHUNKS

    def start_gather(step):
        slot = step % 2
        for c in range(step_chunks):
            src = pl.multiple_of(ytab_ref[step * step_chunks + c] * SUBLANES, SUBLANES)
            pltpu.make_async_copy(y_hbm.at[pl.ds(src, SUBLANES)],
                                  stage.at[slot, pl.ds(c * SUBLANES, SUBLANES)], sem.at[slot]).start()

    @pl.when(i == 0)
    def _():
        start_gather(0)

    @pl.when(i + 1 < pl.num_programs(0))
    def _():
        start_gather(i + 1)

    emb = _rms(_dot(p_ref[...].astype(BF16), wpe_ref[...]), npe_ref[...])

    slot = i % 2
    pltpu.make_async_copy(y_hbm.at[pl.ds(0, stage.shape[1])], stage.at[slot], sem.at[slot]).wait()

    moe = []
    for s in range(MOE_STEP_TILES):
        meta = meta_ref[s * TM_MOE:(s + 1) * TM_MOE, :]
        s1, s2 = _stage_selectors(meta)
        sel = (jnp.where(s1, meta[:, 0:1], 0.0) + jnp.where(s2, meta[:, 1:2], 0.0)).astype(BF16)
        moe.append(_dot(sel, _unpack_pairs(stage[slot, s * STAGE_ROWS:(s + 1) * STAGE_ROWS, :])))
    h2 = h_ref[...] + jnp.concatenate(moe, axis=0)
    gate = _sigmoid(_dot(_rms(h2, npg_ref[...]).astype(BF16), wpg_ref[...]))
    out = h2 + gate * emb
    if final:
        out = _rms(out, nf_ref[...])
    o_ref[...] = out


def _combine(layer, ytab, h, meta, p, wpe, npe, npg, wpg, nf, y, final):
    t, d = h.shape
    k = MOE_STEP_TILES
    tm = k * TM_MOE
    tile = lambda r, w: pl.BlockSpec((r, w), lambda i, *s: (i, 0))
    return pl.pallas_call(
        functools.partial(_combine_kernel, final=final),
        grid_spec=pltpu.PrefetchScalarGridSpec(
            num_scalar_prefetch=1,
            grid=(t // tm,),
            in_specs=[
                tile(tm, d), tile(tm, ROUTER_LANES),
                pl.BlockSpec((None, None, tm, p.shape[-1]), lambda i, *s: (layer, 0, i, 0)),
                _layer_spec(wpe, layer), _layer_spec(npe, layer), _layer_spec(npg, layer), _layer_spec(wpg, layer),
                pl.BlockSpec(nf.shape, lambda i, *s: (0, 0)),
                pl.BlockSpec(memory_space=pl.ANY),
            ],
            out_specs=tile(tm, d),
            scratch_shapes=[pltpu.VMEM((2, k * STAGE_ROWS, y.shape[1]), I32), pltpu.SemaphoreType.DMA((2,))],
        ),
        out_shape=jax.ShapeDtypeStruct((t, d), F32),
        compiler_params=pltpu.CompilerParams(dimension_semantics=("arbitrary",), vmem_limit_bytes=VMEM_LIMIT),
        name="combine",
    )(ytab, h, meta, p, wpe, npe, npg, wpg, nf, y)


def _routing_tables(tab, t):
    nt = t // TM_MOE
    tab = tab.reshape(nt, SUBLANES, ROUTER_LANES)[:, :, EXPERT_LANE0:EXPERT_LANE0 + N_EXPERTS]
    counts = tab[:, 0].astype(I32)
    stage_chunk = tab[:, 1].astype(I32) // SUBLANES
    nch = (counts + SUBLANES - 1) // SUBLANES
    per_block = BM_EXP // SUBLANES
    seg = jnp.sum(nch, axis=0)
    padded = (seg + per_block - 1) // per_block * per_block
    pend = jnp.cumsum(padded)
    pstart = pend - padded
    run_chunk = pstart[None, :] + jnp.cumsum(nch, axis=0) - nch
    nb = -(-(TOP_K * t + (SUBLANES - 1) * nt * N_EXPERTS) // BM_EXP) + N_EXPERTS
    bstart = jnp.arange(nb, dtype=I32) * per_block
    block_e = jnp.minimum(jnp.sum((pend[None, :] <= bstart[:, None]).astype(I32), axis=1), N_EXPERTS - 1)
    n_used = pend[-1:] // per_block
    tile_chunk0 = jnp.arange(nt, dtype=I32)[:, None] * STAGE_CHUNKS
    return nch, tile_chunk0 + stage_chunk, run_chunk, block_e.astype(I32), n_used.astype(I32)


def kernel(x, p, norm_mix, w_in, lb_logits, rec_out_gain, sgu_ln_g, sgu_ln_b, sgu_w, sgu_b, w_out, norm_ffn, w_rg, b_rg, w_re, b_re, w_gate, w_up, w_down, norm_pg, w_pg, w_pe, norm_pe, norm_f):
    bsz, seq, d = x.shape
    depth = w_in.shape[0]
    assert bsz == 1 and seq % (MOE_STEP_TILES * TM_MOE) == 0 and seq % TM_MIX == 0
    h = x.reshape(seq, d)

    lb = jnp.cumsum(jax.nn.softmax(lb_logits.astype(F32), axis=0), axis=0)
    lb = lb - lb[0]
    rows = lambda v: v.reshape(depth, 1, -1)
    la, l1, oml = rows(jnp.log(lb)), rows(jnp.log1p(-lb)), rows(1.0 - lb)
    gain = rows(jnp.tile(rec_out_gain, (1, H_REC)))
    sb = jnp.repeat(jnp.swapaxes(sgu_b, 1, 2), D_SGU // H_SGU, axis=2)
    win_b, wout_b, wpg_b, wpe_b = (w.astype(BF16) for w in (w_in, w_out, w_pg, w_pe))
    npad = ROUTER_LANES - N_GROUPS - N_EXPERTS
    wr = jnp.concatenate([w_rg, w_re, jnp.zeros((depth, d, npad), F32)], axis=2)
    wr_hi = wr.astype(BF16)
    wr = jnp.concatenate([wr_hi, (wr - wr_hi.astype(F32)).astype(BF16)], axis=2)
    br = rows(jnp.concatenate([b_rg, b_re, jnp.zeros((depth, npad), F32)], axis=1))
    wsel = jnp.asarray(np.tile(_decay_selectors(), (1, 3)), BF16)
    lev = jnp.asarray(_pair_levels())
    gmix, gffn, gpg, gpe, lng, lnb = (rows(v) for v in (norm_mix, norm_ffn, norm_pg, norm_pe, sgu_ln_g, sgu_ln_b))
    nf = norm_f.reshape(1, d)

    for l in range(depth):
        h = _mixer(l, h, gmix, win_b, la, l1, oml, gain, lng, lnb, sgu_w, sb, wout_b, wsel, lev)
        xs, meta, tab = _router(l, h, gffn, wr, br)
        nch, xs_chunk, run_chunk, block_e, n_used = _routing_tables(tab, seq)
        ctab, ytab = _chunk_tables(block_e, nch, xs_chunk, run_chunk)
        y = _experts(l, block_e, n_used, ctab, xs, w_gate, w_up, w_down)
        h = _combine(l, ytab, h, meta, p, wpe_b, gpe, gpg, wpg_b, nf, y, final=(l == depth - 1))
    return h.reshape(bsz, seq, d)
```

```python
import functools

import numpy as np
import jax
import jax.numpy as jnp
from jax import lax
from jax.experimental import pallas as pl
from jax.experimental.pallas import tpu as pltpu

F32 = jnp.float32
BF16 = jnp.bfloat16
I32 = jnp.int32

EPS = 1e-6
D_REC = 512
D_SGU = 512
H_REC = 4
DK_REC = 128
CHUNK_REC = 64
SGU_CHUNK = 128
H_SGU = 4
N_GROUPS = 4
EXP_PER_GROUP = 8
N_EXPERTS = N_GROUPS * EXP_PER_GROUP
TOP_K = 2
ROUTER_LANES = 128
EXPERT_LANE0 = N_GROUPS
SUBLANES = 8

TM_MIX = 512
TM_MOE = 256
MOE_STEP_TILES = 4
STAGE_ROWS = 768
BM_EXP = 512
VMEM_LIMIT = 48 * 1024 * 1024

assert STAGE_ROWS >= TOP_K * TM_MOE + (SUBLANES - 1) * N_EXPERTS + SUBLANES
STAGE_CHUNKS = STAGE_ROWS // SUBLANES
ZERO_CHUNK = STAGE_CHUNKS - 1
HIGH_HALF = -65536

_LEVELS = (2, 4, 8, 16, 32, 64)


def _decay_selectors():
    t = np.arange(CHUNK_REC)[:, None]
    s = np.arange(CHUNK_REC)[None, :]
    mats = [((t // m == s // m) & (s <= t)) for m in _LEVELS]
    mats += [((t // m == s // m) & (s > t)) for m in _LEVELS if m < SUBLANES]
    return np.concatenate(mats, axis=0).astype(np.float32)


def _pair_levels():
    t = np.arange(CHUNK_REC)[:, None]
    s = np.arange(CHUNK_REC)[None, :]
    x = t ^ s
    lev = np.floor(np.log2(np.maximum(x, 1))).astype(np.int32)
    lev = np.where(t == s, -1, lev)
    lev = np.where(s > t, -2, lev)
    return lev.astype(np.int32)


def _rms(x, g):
    return x * lax.rsqrt(jnp.mean(x * x, axis=-1, keepdims=True) + EPS) * g


def _sigmoid(x):
    return 0.5 * jnp.tanh(0.5 * x) + 0.5


def _gelu_tanh(x):
    return x * (0.5 * (1.0 + jnp.tanh(0.7978845608028654 * (x + 0.044715 * (x * x * x)))))


def _dot(a, b):
    return jnp.dot(a, b, preferred_element_type=F32)


def _dot_nt(a, b):
    return lax.dot_general(a, b, (((1,), (1,)), ((), ())), preferred_element_type=F32)


def _dot_tn(a, b):
    return lax.dot_general(a, b, (((0,), (0,)), ((), ())), preferred_element_type=F32)


def _pack_pairs(x):
    bits = lax.bitcast_convert_type(x, I32)
    half = x.shape[1] // 2
    return lax.shift_right_logical(bits[:, :half], 16) | (bits[:, half:] & HIGH_HALF)


def _unpack_pairs(w):
    lo = lax.bitcast_convert_type(lax.shift_left(w, 16), F32)
    hi = lax.bitcast_convert_type(w & HIGH_HALF, F32)
    return jnp.concatenate([lo, hi], axis=1).astype(BF16)


def _layer_spec(arr, layer):
    zeros = (0,) * (arr.ndim - 1)
    return pl.BlockSpec((None,) + arr.shape[1:], lambda *a: (layer,) + zeros)


def _mixer_kernel(h_ref, gmix_ref, win_ref, la_ref, l1_ref, oml_ref, gain_ref, lng_ref, lnb_ref,
                  ws_ref, sb_ref, wout_ref, wsel_ref, lev_ref, o_ref, st_ref, ocat_ref):
    tm = h_ref.shape[0]

    @pl.when(pl.program_id(0) == 0)
    def _():
        st_ref[...] = jnp.zeros_like(st_ref)

    h = h_ref[...]
    hb = _rms(h, gmix_ref[...]).astype(BF16)

    def proj(j):
        return _dot(hb, win_ref[:, j * 512:(j + 1) * 512])

    zq = proj(0)
    q = zq * _sigmoid(zq)
    fz = proj(1)
    iv = proj(2).astype(BF16)
    zg = proj(3)
    gate = zg * _sigmoid(zg)

    ls = jnp.minimum(fz, 0.0) - jnp.log1p(jnp.exp(-jnp.abs(fz)))
    c = l1_ref[...] + ls
    la = la_ref[...]
    lf = jnp.maximum(la, c) + jnp.log1p(jnp.exp(-jnp.abs(la - c)))
    kk = oml_ref[...] * _sigmoid(-fz)

    wsel = wsel_ref[...]
    lev = lev_ref[...]
    gain = gain_ref[...]
    nlev = len(_LEVELS)
    states = [st_ref[hh] for hh in range(H_REC)]

    zu = proj(4)
    zv = proj(5)
    r128 = lax.broadcasted_iota(I32, (SGU_CHUNK, SGU_CHUNK), 0)
    c128 = lax.broadcasted_iota(I32, (SGU_CHUNK, SGU_CHUNK), 1)
    wtri = [jnp.where(c128 <= r128, ws_ref[hh], 0.0).astype(BF16) for hh in range(H_SGU)]
    sgu_act = {}

    def sgu_activations(c2):
        sl2 = slice(c2 * SGU_CHUNK, (c2 + 1) * SGU_CHUNK)
        ua = _gelu_tanh(zu[sl2])
        va = _gelu_tanh(zv[sl2])
        mu = jnp.mean(va, axis=-1, keepdims=True)
        vc = va - mu
        var = jnp.mean(vc * vc, axis=-1, keepdims=True)
        sgu_act[c2] = (ua, (vc * lax.rsqrt(var + EPS) * lng_ref[...] + lnb_ref[...]).astype(BF16))

    def sgu_mix(c2):
        sl2 = slice(c2 * SGU_CHUNK, (c2 + 1) * SGU_CHUNK)
        ua, vn = sgu_act.pop(c2)
        for hh in range(H_SGU):
            hs = slice(hh * 128, (hh + 1) * 128)
            s = _dot(wtri[hh], vn[:, hs]) + sb_ref[:, hs]
            ocat_ref[sl2, D_REC + hh * 128:D_REC + (hh + 1) * 128] = (ua[:, hs] * s).astype(BF16)

    rec_per_sgu = SGU_CHUNK // CHUNK_REC

    for cidx in range(tm // CHUNK_REC):
        sl = slice(cidx * CHUNK_REC, (cidx + 1) * CHUNK_REC)
        lf_c = lf[sl]
        hi = lf_c.astype(BF16)
        rem = lf_c - hi.astype(F32)
        mid = rem.astype(BF16)
        lo = (rem - mid.astype(F32)).astype(BF16)
        lsum = _dot(wsel, jnp.concatenate([hi, mid, lo], axis=0))
        pd = jnp.exp(lsum[:nlev * 64])
        pe_small = jnp.exp(lsum[nlev * 64:])
        pe = []
        for j, m in enumerate(_LEVELS):
            if m < SUBLANES:
                pe.append(pe_small[j * 64:(j + 1) * 64])
            else:
                blk = lsum[j * 64:(j + 1) * 64].reshape(CHUNK_REC // m, m, D_REC)
                pe.append(jnp.exp((blk[:, m - 1:m, :] - blk).reshape(CHUNK_REC, D_REC)))
        f_c = jnp.exp(lf_c)
        for hh in range(H_REC):
            hs = slice(hh * DK_REC, (hh + 1) * DK_REC)
            q_h = q[sl, hs]
            k_h = kk[sl, hs]
            i_h = iv[sl, hs]
            k_b = k_h.astype(BF16)
            sc = jnp.where(lev == -1, _dot_nt(q_h.astype(BF16), k_b), 0.0)
            sc = sc + jnp.where(lev == 0, _dot_nt((q_h * f_c[:, hs]).astype(BF16), k_b), 0.0)
            for j in range(nlev - 1):
                qd = (q_h * pd[j * 64:(j + 1) * 64, hs]).astype(BF16)
                kd = (k_h * pe[j][:, hs]).astype(BF16)
                sc = sc + jnp.where(lev == j + 1, _dot_nt(qd, kd), 0.0)
            jl = nlev - 1
            qb = (q_h * pd[jl * 64:(jl + 1) * 64, hs]).astype(BF16)
            kd = (k_h * pe[jl][:, hs]).astype(BF16)
            st = states[hh]
            o = _dot(sc.astype(BF16), i_h) + _dot_nt(qb, st.astype(BF16))
            dec = pd[jl * 64 + 63:jl * 64 + 64, hs]
            states[hh] = st * dec + _dot_tn(i_h, kd)
            o = o * lax.rsqrt(jnp.mean(o * o, axis=-1, keepdims=True) + EPS) * gain[:, hs]
            ocat_ref[sl, hs] = (o * gate[sl, hs]).astype(BF16)
        if cidx % rec_per_sgu == 0:
            sgu_activations(cidx // rec_per_sgu)
        if cidx % rec_per_sgu == rec_per_sgu - 1:
            sgu_mix(cidx // rec_per_sgu)

    for hh in range(H_REC):
        st_ref[hh] = states[hh]

    o_ref[...] = h + _dot(ocat_ref[...], wout_ref[...])


def _mixer(layer, h, gmix, win, la, l1, oml, gain, lng, lnb, ws, sb, wout, wsel, lev):
    t, d = h.shape
    tm = min(TM_MIX, t)
    per_layer = (gmix, win, la, l1, oml, gain, lng, lnb, ws, sb, wout)
    return pl.pallas_call(
        _mixer_kernel,
        grid=(t // tm,),
        in_specs=[pl.BlockSpec((tm, d), lambda i: (i, 0))]
        + [_layer_spec(a, layer) for a in per_layer]
        + [pl.BlockSpec(wsel.shape, lambda i: (0, 0)), pl.BlockSpec(lev.shape, lambda i: (0, 0))],
        out_specs=pl.BlockSpec((tm, d), lambda i: (i, 0)),
        out_shape=jax.ShapeDtypeStruct((t, d), F32),
        scratch_shapes=[pltpu.VMEM((H_REC, DK_REC, DK_REC), F32), pltpu.VMEM((tm, d), BF16)],
        compiler_params=pltpu.CompilerParams(dimension_semantics=("arbitrary",), vmem_limit_bytes=VMEM_LIMIT),
        name="mixer",
    )(h, *per_layer, wsel, lev)


def _stage_selectors(meta):
    col = lax.broadcasted_iota(I32, (meta.shape[0], STAGE_ROWS), 1)
    return [col == meta[:, 2 + k:3 + k].astype(I32) for k in range(TOP_K)]


def _route_tile(logits, biased, xh):
    tm = logits.shape[0]
    lane = lax.broadcasted_iota(I32, (tm, ROUTER_LANES), 1).astype(F32)
    ninf = -jnp.inf
    big = float(ROUTER_LANES)

    def first_argmax(v):
        m = jnp.max(v, axis=-1, keepdims=True)
        return jnp.min(jnp.where(v == m, lane, big), axis=-1, keepdims=True)

    def pick(v, idx):
        return jnp.sum(jnp.where(lane == idx, v, 0.0), axis=-1, keepdims=True)

    is_g = lane < float(N_GROUPS)
    g_sel = first_argmax(jnp.where(is_g, biased, ninf))
    lg = jnp.where(is_g, logits, ninf)
    eg = jnp.exp(lg - jnp.max(lg, axis=-1, keepdims=True))
    g_w = pick(eg, g_sel) / jnp.sum(eg, axis=-1, keepdims=True)

    lo = float(EXPERT_LANE0) + float(EXP_PER_GROUP) * g_sel
    is_e = (lane >= lo) & (lane < lo + float(EXP_PER_GROUP))
    eb = jnp.where(is_e, biased, ninf)
    i1 = first_argmax(eb)
    i2 = first_argmax(jnp.where(lane == i1, ninf, eb))
    le = jnp.where(is_e, logits, ninf)
    ee = jnp.exp(le - jnp.max(le, axis=-1, keepdims=True))
    p1 = pick(ee, i1)
    p2 = pick(ee, i2)
    w1 = g_w * (p1 / (p1 + p2))
    w2 = g_w * (p2 / (p1 + p2))

    hot1 = lane == i1
    hot2 = lane == i2
    cnt = jnp.where(hot1 | hot2, 1.0, 0.0)
    row = lax.broadcasted_iota(I32, (tm, tm), 0)
    col = lax.broadcasted_iota(I32, (tm, tm), 1)
    ltri = jnp.where(col < row, 1.0, 0.0).astype(BF16)
    rank = _dot(ltri, cnt.astype(BF16))

    n_row = jnp.sum(cnt, axis=0, keepdims=True)
    chunks = jnp.floor((n_row + float(SUBLANES - 1)) * (1.0 / SUBLANES))
    lrow = lax.broadcasted_iota(I32, (ROUTER_LANES, ROUTER_LANES), 0)
    lcol = lax.broadcasted_iota(I32, (ROUTER_LANES, ROUTER_LANES), 1)
    before = jnp.where(lrow < lcol, 1.0, 0.0).astype(BF16)
    chunks8 = jnp.broadcast_to(chunks, (SUBLANES, ROUTER_LANES))
    so_rows = _dot(chunks8.astype(BF16), before) * float(SUBLANES)
    srow = lax.broadcasted_iota(I32, (SUBLANES, ROUTER_LANES), 0)
    tab = jnp.where(srow == 0, jnp.broadcast_to(n_row, (SUBLANES, ROUTER_LANES)), so_rows)

    place = rank + so_rows[0:1]
    pos1 = jnp.sum(jnp.where(hot1, place, 0.0), axis=-1, keepdims=True)
    pos2 = jnp.sum(jnp.where(hot2, place, 0.0), axis=-1, keepdims=True)
    meta = jnp.where(lane == 0.0, w1, 0.0)
    for j, v in enumerate((w2, pos1, pos2), start=1):
        meta = jnp.where(lane == float(j), v, meta)

    meta_t = jnp.transpose(meta)
    stage_row = lax.broadcasted_iota(I32, (STAGE_ROWS, tm), 0)
    hit = (stage_row == meta_t[2:3, :].astype(I32)) | (stage_row == meta_t[3:4, :].astype(I32))
    sel = jnp.where(hit, 1.0, 0.0).astype(BF16)
    return tab, meta, _pack_pairs(_dot(sel, xh))


def _router_kernel(h_ref, g_ref, wr_ref, br_ref, xs_ref, meta_ref, tab_ref):
    xn = _rms(h_ref[...], g_ref[...])
    xh = xn.astype(BF16)
    xl = (xn - xh.astype(F32)).astype(BF16)
    part = _dot(xh, wr_ref[...])
    logits = part[:, :ROUTER_LANES] + part[:, ROUTER_LANES:] + _dot(xl, wr_ref[:, :ROUTER_LANES])
    biased = logits + br_ref[...]
    for s in range(MOE_STEP_TILES):
        rows = slice(s * TM_MOE, (s + 1) * TM_MOE)
        tab, meta, xs = _route_tile(logits[rows], biased[rows], xh[rows])
        tab_ref[s * SUBLANES:(s + 1) * SUBLANES, :] = tab
        meta_ref[rows, :] = meta
        xs_ref[s * STAGE_ROWS:(s + 1) * STAGE_ROWS, :] = xs


def _router(layer, h, g, wr, br):
    t, d = h.shape
    nt = t // TM_MOE
    k = MOE_STEP_TILES
    return pl.pallas_call(
        _router_kernel,
        grid=(nt // k,),
        in_specs=[pl.BlockSpec((k * TM_MOE, d), lambda i: (i, 0))] + [_layer_spec(a, layer) for a in (g, wr, br)],
        out_specs=[
            pl.BlockSpec((k * STAGE_ROWS, d // 2), lambda i: (i, 0)),
            pl.BlockSpec((k * TM_MOE, ROUTER_LANES), lambda i: (i, 0)),
            pl.BlockSpec((k * SUBLANES, ROUTER_LANES), lambda i: (i, 0)),
        ],
        out_shape=[
            jax.ShapeDtypeStruct((nt * STAGE_ROWS, d // 2), I32),
            jax.ShapeDtypeStruct((t, ROUTER_LANES), F32),
            jax.ShapeDtypeStruct((nt * SUBLANES, ROUTER_LANES), F32),
        ],
        compiler_params=pltpu.CompilerParams(dimension_semantics=("arbitrary",), vmem_limit_bytes=VMEM_LIMIT),
        name="router",
    )(h, g, wr, br)


def _match_runs(query, key, n, val, default):
    d = query - key
    inside = (d >= 0.0) & (d < n)
    hit = jnp.sum(jnp.where(inside, 1.0, 0.0), axis=0, keepdims=True)
    got = jnp.sum(jnp.where(inside, val + d, 0.0), axis=0, keepdims=True)
    return jnp.where(hit > 0.0, got, float(default)).astype(I32)


def _chunk_table_kernel(be_ref, tkey_ref, tn_ref, tval_ref, ekey_ref, en_ref, eval_ref, ctab_ref, ytab_ref):
    lanes = ytab_ref.shape[1]
    nruns = tkey_ref.shape[0]
    lane = lax.broadcasted_iota(I32, (1, lanes), 1).astype(F32)
    tiles_met = -(-lanes // STAGE_CHUNKS) + 1
    for k in range(ytab_ref.shape[0]):
        r0 = min((k * lanes) // STAGE_CHUNKS * N_EXPERTS, nruns - tiles_met * N_EXPERTS)
        rs = slice(r0, r0 + tiles_met * N_EXPERTS)
        ytab_ref[k:k + 1, :] = _match_runs(lane + float(k * lanes), tkey_ref[rs, :], tn_ref[rs, :],
                                           tval_ref[rs, :], 0)

    per_block = ctab_ref.shape[1]
    per_expert = nruns // N_EXPERTS
    lane_b = lane[:, :per_block]

    def block(b, carry):
        r0 = pl.multiple_of(be_ref[b] * per_expert, per_expert)
        rs = pl.ds(r0, per_expert)
        query = lane_b + lax.convert_element_type(b * per_block, F32)
        ctab_ref[pl.ds(b, 1), :] = _match_runs(query, ekey_ref[rs, :per_block], en_ref[rs, :per_block],
                                               eval_ref[rs, :per_block], ZERO_CHUNK)
        return carry

    lax.fori_loop(0, ctab_ref.shape[0], block, 0)


def _chunk_tables(block_e, nch, xs_chunk, run_chunk):
    nt = nch.shape[0]
    nb = block_e.shape[0]
    lanes = ROUTER_LANES
    wide = lambda a: jnp.broadcast_to(a.reshape(-1, 1).astype(F32), (a.size, lanes))
    tile_major = [wide(a) for a in (xs_chunk, nch, run_chunk)]
    expert_major = [wide(a.T) for a in (run_chunk, nch, xs_chunk)]
    full = lambda a: pl.BlockSpec(a.shape, lambda i, be: (0, 0))
    n_stage_rows = nt * STAGE_CHUNKS // lanes
    ctab, ytab = pl.pallas_call(
        _chunk_table_kernel,
        grid_spec=pltpu.PrefetchScalarGridSpec(
            num_scalar_prefetch=1,
            grid=(1,),
            in_specs=[full(a) for a in tile_major + expert_major],
            out_specs=[pl.BlockSpec((nb, BM_EXP // SUBLANES), lambda i, be: (0, 0)),
                       pl.BlockSpec((n_stage_rows, lanes), lambda i, be: (0, 0))],
        ),
        out_shape=[jax.ShapeDtypeStruct((nb, BM_EXP // SUBLANES), I32),
                   jax.ShapeDtypeStruct((n_stage_rows, lanes), I32)],
        name="chunks",
    )(block_e, *tile_major, *expert_major)
    return ctab.reshape(-1), ytab.reshape(-1)


def _expert_kernel(be_ref, nu_ref, plan_ref, tab_ref, xs_hbm, wg_hbm, wu_hbm, wd_hbm, y_ref,
                   xbuf, wgbuf, wubuf, wdbuf, sem, wsem, *, layer):
    b = pl.program_id(0)
    nu = nu_ref[0]
    bm = xbuf.shape[1]
    per_block = bm // SUBLANES

    def start_gather(blk):
        slot = blk % 2
        for c in range(per_block):
            src = pl.multiple_of(tab_ref[blk * per_block + c] * SUBLANES, SUBLANES)
            pltpu.make_async_copy(xs_hbm.at[pl.ds(src, SUBLANES)],
                                  xbuf.at[slot, pl.ds(c * SUBLANES, SUBLANES)], sem.at[slot]).start()

    def wait_gather(blk):
        slot = blk % 2
        pltpu.make_async_copy(xs_hbm.at[pl.ds(0, bm)], xbuf.at[slot], sem.at[slot]).wait()

    def weight_copies(e, slot):
        return [pltpu.make_async_copy(w_hbm.at[layer, e], buf.at[slot], wsem.at[slot])
                for w_hbm, buf in ((wg_hbm, wgbuf), (wu_hbm, wubuf), (wd_hbm, wdbuf))]

    @pl.when(b == 0)
    def _():
        start_gather(0)
        for cp in weight_copies(be_ref[0], plan_ref[0]):
            cp.start()

    @pl.when(b + 1 < nu)
    def _():
        start_gather(b + 1)

    @pl.when(b < nu)
    def _():
        wslot = plan_ref[3 * b]

        @pl.when(plan_ref[3 * b + 1] == 1)
        def _():
            for cp in weight_copies(be_ref[b], wslot):
                cp.wait()
            nxt = plan_ref[3 * b + 2]

            @pl.when(nxt >= 0)
            def _():
                for cp in weight_copies(nxt, 1 - wslot):
                    cp.start()

        wait_gather(b)
        x = _unpack_pairs(xbuf[b % 2])
        a = _dot(x, wgbuf[wslot].astype(BF16))
        u = _dot(x, wubuf[wslot].astype(BF16))
        hmid = (a * _sigmoid(a)) * u
        y = _dot(hmid.astype(BF16), wdbuf[wslot].astype(BF16))
        y_ref[...] = _pack_pairs(y.astype(BF16).astype(F32))

    @pl.when(b >= nu)
    def _():
        y_ref[...] = jnp.zeros_like(y_ref)


def _weight_plan(block_e, n_used):
    nb = block_e.shape[0]
    idx = jnp.arange(nb, dtype=I32)
    used = idx < n_used[0]
    first = used & jnp.concatenate([jnp.ones((1,), bool), block_e[1:] != block_e[:-1]])
    slot = (jnp.cumsum(first.astype(I32)) - 1) % 2
    nxt_idx = lax.cummin(jnp.where(first, idx, nb)[::-1])[::-1]
    nxt_idx = jnp.concatenate([nxt_idx[1:], jnp.full((1,), nb, I32)])
    nxt = jnp.where(nxt_idx < nb, block_e[jnp.minimum(nxt_idx, nb - 1)], -1)
    return jnp.stack([slot, first.astype(I32), nxt], axis=1).reshape(-1).astype(I32)


def _experts(layer, block_e, n_used, tab, xs, wg, wu, wd):
    d, de = wg.shape[-2:]
    dp = xs.shape[1]
    nb = block_e.shape[0]
    hbm = pl.BlockSpec(memory_space=pl.ANY)
    return pl.pallas_call(
        functools.partial(_expert_kernel, layer=layer),
        grid_spec=pltpu.PrefetchScalarGridSpec(
            num_scalar_prefetch=4,
            grid=(nb,),
            in_specs=[hbm, hbm, hbm, hbm],
            out_specs=pl.BlockSpec((BM_EXP, dp), lambda b, *s: (b, 0)),
            scratch_shapes=[
                pltpu.VMEM((2, BM_EXP, dp), I32),
                pltpu.VMEM((2, d, de), F32), pltpu.VMEM((2, d, de), F32), pltpu.VMEM((2, de, d), F32),
                pltpu.SemaphoreType.DMA((2,)), pltpu.SemaphoreType.DMA((2,)),
            ],
        ),
        out_shape=jax.ShapeDtypeStruct((nb * BM_EXP, dp), I32),
        compiler_params=pltpu.CompilerParams(dimension_semantics=("arbitrary",), vmem_limit_bytes=VMEM_LIMIT),
        name="experts",
    )(block_e, n_used, _weight_plan(block_e, n_used), tab, xs, wg, wu, wd)


def _combine_kernel(ytab_ref, h_ref, meta_ref, p_ref, wpe_ref, npe_ref, npg_ref,
                    wpg_ref, nf_ref, y_hbm, o_ref, stage, sem, *, final):
    i = pl.program_id(0)

    step_chunks = MOE_STEP_TILES * STAGE_CHUNKS

    def start_gather(step):
        slot = step % 2
        for c in range(step_chunks):
            src = pl.multiple_of(ytab_ref[step * step_chunks + c] * SUBLANES, SUBLANES)
            pltpu.make_async_copy(y_hbm.at[pl.ds(src, SUBLANES)],
                                  stage.at[slot, pl.ds(c * SUBLANES, SUBLANES)], sem.at[slot]).start()

    @pl.when(i == 0)
    def _():
        start_gather(0)

    @pl.when(i + 1 < pl.num_programs(0))
    def _():
        start_gather(i + 1)

    emb = _rms(_dot(p_ref[...].astype(BF16), wpe_ref[...]), npe_ref[...])

    slot = i % 2
    pltpu.make_async_copy(y_hbm.at[pl.ds(0, stage.shape[1])], stage.at[slot], sem.at[slot]).wait()

    moe = []
    for s in range(MOE_STEP_TILES):
        meta = meta_ref[s * TM_MOE:(s + 1) * TM_MOE, :]
        s1, s2 = _stage_selectors(meta)
        sel = (jnp.where(s1, meta[:, 0:1], 0.0) + jnp.where(s2, meta[:, 1:2], 0.0)).astype(BF16)
        moe.append(_dot(sel, _unpack_pairs(stage[slot, s * STAGE_ROWS:(s + 1) * STAGE_ROWS, :])))
    h2 = h_ref[...] + jnp.concatenate(moe, axis=0)
    gate = _sigmoid(_dot(_rms(h2, npg_ref[...]).astype(BF16), wpg_ref[...]))
    out = h2 + gate * emb
    if final:
        out = _rms(out, nf_ref[...])
    o_ref[...] = out


def _combine(layer, ytab, h, meta, p, wpe, npe, npg, wpg, nf, y, final):
    t, d = h.shape
    k = MOE_STEP_TILES
    tm = k * TM_MOE
    tile = lambda r, w: pl.BlockSpec((r, w), lambda i, *s: (i, 0))
    return pl.pallas_call(
        functools.partial(_combine_kernel, final=final),
        grid_spec=pltpu.PrefetchScalarGridSpec(
            num_scalar_prefetch=1,
            grid=(t // tm,),
            in_specs=[
                tile(tm, d), tile(tm, ROUTER_LANES),
                pl.BlockSpec((None, None, tm, p.shape[-1]), lambda i, *s: (layer, 0, i, 0)),
                _layer_spec(wpe, layer), _layer_spec(npe, layer), _layer_spec(npg, layer), _layer_spec(wpg, layer),
                pl.BlockSpec(nf.shape, lambda i, *s: (0, 0)),
                pl.BlockSpec(memory_space=pl.ANY),
            ],
            out_specs=tile(tm, d),
            scratch_shapes=[pltpu.VMEM((2, k * STAGE_ROWS, y.shape[1]), I32), pltpu.SemaphoreType.DMA((2,))],
        ),
        out_shape=jax.ShapeDtypeStruct((t, d), F32),
        compiler_params=pltpu.CompilerParams(dimension_semantics=("arbitrary",), vmem_limit_bytes=VMEM_LIMIT),
        name="combine",
    )(ytab, h, meta, p, wpe, npe, npg, wpg, nf, y)


def _routing_tables(tab, t):
    nt = t // TM_MOE
    tab = tab.reshape(nt, SUBLANES, ROUTER_LANES)[:, :, EXPERT_LANE0:EXPERT_LANE0 + N_EXPERTS]
    counts = tab[:, 0].astype(I32)
    stage_chunk = tab[:, 1].astype(I32) // SUBLANES
    nch = (counts + SUBLANES - 1) // SUBLANES
    per_block = BM_EXP // SUBLANES
    seg = jnp.sum(nch, axis=0)
    padded = (seg + per_block - 1) // per_block * per_block
    pend = jnp.cumsum(padded)
    pstart = pend - padded
    run_chunk = pstart[None, :] + jnp.cumsum(nch, axis=0) - nch
    nb = -(-(TOP_K * t + (SUBLANES - 1) * nt * N_EXPERTS) // BM_EXP) + N_EXPERTS
    bstart = jnp.arange(nb, dtype=I32) * per_block
    block_e = jnp.minimum(jnp.sum((pend[None, :] <= bstart[:, None]).astype(I32), axis=1), N_EXPERTS - 1)
    n_used = pend[-1:] // per_block
    tile_chunk0 = jnp.arange(nt, dtype=I32)[:, None] * STAGE_CHUNKS
    return nch, tile_chunk0 + stage_chunk, run_chunk, block_e.astype(I32), n_used.astype(I32)


def kernel(x, p, norm_mix, w_in, lb_logits, rec_out_gain, sgu_ln_g, sgu_ln_b, sgu_w, sgu_b, w_out, norm_ffn, w_rg, b_rg, w_re, b_re, w_gate, w_up, w_down, norm_pg, w_pg, w_pe, norm_pe, norm_f):
    bsz, seq, d = x.shape
    depth = w_in.shape[0]
    assert bsz == 1 and seq % (MOE_STEP_TILES * TM_MOE) == 0 and seq % TM_MIX == 0
    h = x.reshape(seq, d)

    lb = jnp.cumsum(jax.nn.softmax(lb_logits.astype(F32), axis=0), axis=0)
    lb = lb - lb[0]
    rows = lambda v: v.reshape(depth, 1, -1)
    la, l1, oml = rows(jnp.log(lb)), rows(jnp.log1p(-lb)), rows(1.0 - lb)
    gain = rows(jnp.tile(rec_out_gain, (1, H_REC)))
    sb = jnp.repeat(jnp.swapaxes(sgu_b, 1, 2), D_SGU // H_SGU, axis=2)
    win_b, wout_b, wpg_b, wpe_b = (w.astype(BF16) for w in (w_in, w_out, w_pg, w_pe))
    npad = ROUTER_LANES - N_GROUPS - N_EXPERTS
    wr = jnp.concatenate([w_rg, w_re, jnp.zeros((depth, d, npad), F32)], axis=2)
    wr_hi = wr.astype(BF16)
    wr = jnp.concatenate([wr_hi, (wr - wr_hi.astype(F32)).astype(BF16)], axis=2)
    br = rows(jnp.concatenate([b_rg, b_re, jnp.zeros((depth, npad), F32)], axis=1))
    wsel = jnp.asarray(np.tile(_decay_selectors(), (1, 3)), BF16)
    lev = jnp.asarray(_pair_levels())
    gmix, gffn, gpg, gpe, lng, lnb = (rows(v) for v in (norm_mix, norm_ffn, norm_pg, norm_pe, sgu_ln_g, sgu_ln_b))
    nf = norm_f.reshape(1, d)

    for l in range(depth):
        h = _mixer(l, h, gmix, win_b, la, l1, oml, gain, lng, lnb, sgu_w, sb, wout_b, wsel, lev)
        xs, meta, tab = _router(l, h, gffn, wr, br)
        nch, xs_chunk, run_chunk, block_e, n_used = _routing_tables(tab, seq)
        ctab, ytab = _chunk_tables(block_e, nch, xs_chunk, run_chunk)
        y = _experts(l, block_e, n_used, ctab, xs, w_gate, w_up, w_down)
        h = _combine(l, ytab, h, meta, p, wpe_b, gpe, gpg, wpg_b, nf, y, final=(l == depth - 1))
    return h.reshape(bsz, seq, d)
```

```python
import functools

import numpy as np
import jax
import jax.numpy as jnp
from jax import lax
from jax.experimental import pallas as pl
from jax.experimental.pallas import tpu as pltpu

F32 = jnp.float32
BF16 = jnp.bfloat16
I32 = jnp.int32

EPS = 1e-6
D_REC = 512
D_SGU = 512
H_REC = 4
DK_REC = 128
CHUNK_REC = 64
SGU_CHUNK = 128
H_SGU = 4
N_GROUPS = 4
EXP_PER_GROUP = 8
N_EXPERTS = N_GROUPS * EXP_PER_GROUP
TOP_K = 2
ROUTER_LANES = 128
EXPERT_LANE0 = N_GROUPS
SUBLANES = 8

TM_MIX = 512
TM_MOE = 256
MOE_STEP_TILES = 4
STAGE_ROWS = 768
BM_EXP = 512
VMEM_LIMIT = 48 * 1024 * 1024

assert STAGE_ROWS >= TOP_K * TM_MOE + (SUBLANES - 1) * N_EXPERTS + SUBLANES
STAGE_CHUNKS = STAGE_ROWS // SUBLANES
ZERO_CHUNK = STAGE_CHUNKS - 1
HIGH_HALF = -65536

_LEVELS = (2, 4, 8, 16, 32, 64)


def _decay_selectors():
    t = np.arange(CHUNK_REC)[:, None]
    s = np.arange(CHUNK_REC)[None, :]
    mats = [((t // m == s // m) & (s <= t)) for m in _LEVELS]
    mats += [((t // m == s // m) & (s > t)) for m in _LEVELS if m < SUBLANES]
    return np.concatenate(mats, axis=0).astype(np.float32)


def _pair_levels():
    t = np.arange(CHUNK_REC)[:, None]
    s = np.arange(CHUNK_REC)[None, :]
    x = t ^ s
    lev = np.floor(np.log2(np.maximum(x, 1))).astype(np.int32)
    lev = np.where(t == s, -1, lev)
    lev = np.where(s > t, -2, lev)
    return lev.astype(np.int32)


def _rms(x, g):
    return x * lax.rsqrt(jnp.mean(x * x, axis=-1, keepdims=True) + EPS) * g


def _sigmoid(x):
    return 0.5 * jnp.tanh(0.5 * x) + 0.5


def _gelu_tanh(x):
    return x * (0.5 * (1.0 + jnp.tanh(0.7978845608028654 * (x + 0.044715 * (x * x * x)))))


def _dot(a, b):
    return jnp.dot(a, b, preferred_element_type=F32)


def _dot_nt(a, b):
    return lax.dot_general(a, b, (((1,), (1,)), ((), ())), preferred_element_type=F32)


def _dot_tn(a, b):
    return lax.dot_general(a, b, (((0,), (0,)), ((), ())), preferred_element_type=F32)


def _pack_pairs(x):
    bits = lax.bitcast_convert_type(x, I32)
    half = x.shape[1] // 2
    return lax.shift_right_logical(bits[:, :half], 16) | (bits[:, half:] & HIGH_HALF)


def _unpack_pairs(w):
    lo = lax.bitcast_convert_type(lax.shift_left(w, 16), F32)
    hi = lax.bitcast_convert_type(w & HIGH_HALF, F32)
    return jnp.concatenate([lo, hi], axis=1).astype(BF16)


def _layer_spec(arr, layer):
    zeros = (0,) * (arr.ndim - 1)
    return pl.BlockSpec((None,) + arr.shape[1:], lambda *a: (layer,) + zeros)


def _mixer_kernel(h_ref, gmix_ref, win_ref, la_ref, l1_ref, oml_ref, gain_ref, lng_ref, lnb_ref,
                  ws_ref, sb_ref, wout_ref, wsel_ref, lev_ref, o_ref, st_ref, ocat_ref):
    tm = h_ref.shape[0]

    @pl.when(pl.program_id(0) == 0)
    def _():
        st_ref[...] = jnp.zeros_like(st_ref)

    h = h_ref[...]
    hb = _rms(h, gmix_ref[...]).astype(BF16)

    def proj(j):
        return _dot(hb, win_ref[:, j * 512:(j + 1) * 512])

    zq = proj(0)
    q = zq * _sigmoid(zq)
    fz = proj(1)
    iv = proj(2).astype(BF16)
    zg = proj(3)
    gate = zg * _sigmoid(zg)

    ls = jnp.minimum(fz, 0.0) - jnp.log1p(jnp.exp(-jnp.abs(fz)))
    c = l1_ref[...] + ls
    la = la_ref[...]
    lf = jnp.maximum(la, c) + jnp.log1p(jnp.exp(-jnp.abs(la - c)))
    kk = oml_ref[...] * _sigmoid(-fz)

    wsel = wsel_ref[...]
    lev = lev_ref[...]
    gain = gain_ref[...]
    nlev = len(_LEVELS)
    states = [st_ref[hh] for hh in range(H_REC)]

    zu = proj(4)
    zv = proj(5)
    r128 = lax.broadcasted_iota(I32, (SGU_CHUNK, SGU_CHUNK), 0)
    c128 = lax.broadcasted_iota(I32, (SGU_CHUNK, SGU_CHUNK), 1)
    wtri = [jnp.where(c128 <= r128, ws_ref[hh], 0.0).astype(BF16) for hh in range(H_SGU)]
    sgu_act = {}

    def sgu_activations(c2):
        sl2 = slice(c2 * SGU_CHUNK, (c2 + 1) * SGU_CHUNK)
        ua = _gelu_tanh(zu[sl2])
        va = _gelu_tanh(zv[sl2])
        mu = jnp.mean(va, axis=-1, keepdims=True)
        vc = va - mu
        var = jnp.mean(vc * vc, axis=-1, keepdims=True)
        sgu_act[c2] = (ua, (vc * lax.rsqrt(var + EPS) * lng_ref[...] + lnb_ref[...]).astype(BF16))

    def sgu_mix(c2):
        sl2 = slice(c2 * SGU_CHUNK, (c2 + 1) * SGU_CHUNK)
        ua, vn = sgu_act.pop(c2)
        for hh in range(H_SGU):
            hs = slice(hh * 128, (hh + 1) * 128)
            s = _dot(wtri[hh], vn[:, hs]) + sb_ref[:, hs]
            ocat_ref[sl2, D_REC + hh * 128:D_REC + (hh + 1) * 128] = (ua[:, hs] * s).astype(BF16)

    rec_per_sgu = SGU_CHUNK // CHUNK_REC

    for cidx in range(tm // CHUNK_REC):
        sl = slice(cidx * CHUNK_REC, (cidx + 1) * CHUNK_REC)
        lf_c = lf[sl]
        hi = lf_c.astype(BF16)
        rem = lf_c - hi.astype(F32)
        mid = rem.astype(BF16)
        lo = (rem - mid.astype(F32)).astype(BF16)
        lsum = _dot(wsel, jnp.concatenate([hi, mid, lo], axis=0))
        pd = jnp.exp(lsum[:nlev * 64])
        pe_small = jnp.exp(lsum[nlev * 64:])
        pe = []
        for j, m in enumerate(_LEVELS):
            if m < SUBLANES:
                pe.append(pe_small[j * 64:(j + 1) * 64])
            else:
                blk = lsum[j * 64:(j + 1) * 64].reshape(CHUNK_REC // m, m, D_REC)
                pe.append(jnp.exp((blk[:, m - 1:m, :] - blk).reshape(CHUNK_REC, D_REC)))
        f_c = jnp.exp(lf_c)
        for hh in range(H_REC):
            hs = slice(hh * DK_REC, (hh + 1) * DK_REC)
            q_h = q[sl, hs]
            k_h = kk[sl, hs]
            i_h = iv[sl, hs]
            k_b = k_h.astype(BF16)
            sc = jnp.where(lev == -1, _dot_nt(q_h.astype(BF16), k_b), 0.0)
            sc = sc + jnp.where(lev == 0, _dot_nt((q_h * f_c[:, hs]).astype(BF16), k_b), 0.0)
            for j in range(nlev - 1):
                qd = (q_h * pd[j * 64:(j + 1) * 64, hs]).astype(BF16)
                kd = (k_h * pe[j][:, hs]).astype(BF16)
                sc = sc + jnp.where(lev == j + 1, _dot_nt(qd, kd), 0.0)
            jl = nlev - 1
            qb = (q_h * pd[jl * 64:(jl + 1) * 64, hs]).astype(BF16)
            kd = (k_h * pe[jl][:, hs]).astype(BF16)
            st = states[hh]
            o = _dot(sc.astype(BF16), i_h) + _dot_nt(qb, st.astype(BF16))
            dec = pd[jl * 64 + 63:jl * 64 + 64, hs]
            states[hh] = st * dec + _dot_tn(i_h, kd)
            o = o * lax.rsqrt(jnp.mean(o * o, axis=-1, keepdims=True) + EPS) * gain[:, hs]
            ocat_ref[sl, hs] = (o * gate[sl, hs]).astype(BF16)
        if cidx % rec_per_sgu == 0:
            sgu_activations(cidx // rec_per_sgu)
        if cidx % rec_per_sgu == rec_per_sgu - 1:
            sgu_mix(cidx // rec_per_sgu)

    for hh in range(H_REC):
        st_ref[hh] = states[hh]

    o_ref[...] = h + _dot(ocat_ref[...], wout_ref[...])


def _mixer(layer, h, gmix, win, la, l1, oml, gain, lng, lnb, ws, sb, wout, wsel, lev):
    t, d = h.shape
    tm = min(TM_MIX, t)
    per_layer = (gmix, win, la, l1, oml, gain, lng, lnb, ws, sb, wout)
    return pl.pallas_call(
        _mixer_kernel,
        grid=(t // tm,),
        in_specs=[pl.BlockSpec((tm, d), lambda i: (i, 0))]
        + [_layer_spec(a, layer) for a in per_layer]
        + [pl.BlockSpec(wsel.shape, lambda i: (0, 0)), pl.BlockSpec(lev.shape, lambda i: (0, 0))],
        out_specs=pl.BlockSpec((tm, d), lambda i: (i, 0)),
        out_shape=jax.ShapeDtypeStruct((t, d), F32),
        scratch_shapes=[pltpu.VMEM((H_REC, DK_REC, DK_REC), F32), pltpu.VMEM((tm, d), BF16)],
        compiler_params=pltpu.CompilerParams(dimension_semantics=("arbitrary",), vmem_limit_bytes=VMEM_LIMIT),
        name="mixer",
    )(h, *per_layer, wsel, lev)


def _stage_selectors(meta):
    col = lax.broadcasted_iota(I32, (meta.shape[0], STAGE_ROWS), 1)
    return [col == meta[:, 2 + k:3 + k].astype(I32) for k in range(TOP_K)]


def _route_tile(logits, biased, xh):
    tm = logits.shape[0]
    lane = lax.broadcasted_iota(I32, (tm, ROUTER_LANES), 1).astype(F32)
    ninf = -jnp.inf
    big = float(ROUTER_LANES)

    def first_argmax(v):
        m = jnp.max(v, axis=-1, keepdims=True)
        return jnp.min(jnp.where(v == m, lane, big), axis=-1, keepdims=True)

    def pick(v, idx):
        return jnp.sum(jnp.where(lane == idx, v, 0.0), axis=-1, keepdims=True)

    is_g = lane < float(N_GROUPS)
    g_sel = first_argmax(jnp.where(is_g, biased, ninf))
    lg = jnp.where(is_g, logits, ninf)
    eg = jnp.exp(lg - jnp.max(lg, axis=-1, keepdims=True))
    g_w = pick(eg, g_sel) / jnp.sum(eg, axis=-1, keepdims=True)

    lo = float(EXPERT_LANE0) + float(EXP_PER_GROUP) * g_sel
    is_e = (lane >= lo) & (lane < lo + float(EXP_PER_GROUP))
    eb = jnp.where(is_e, biased, ninf)
    i1 = first_argmax(eb)
    i2 = first_argmax(jnp.where(lane == i1, ninf, eb))
    le = jnp.where(is_e, logits, ninf)
    ee = jnp.exp(le - jnp.max(le, axis=-1, keepdims=True))
    p1 = pick(ee, i1)
    p2 = pick(ee, i2)
    w1 = g_w * (p1 / (p1 + p2))
    w2 = g_w * (p2 / (p1 + p2))

    hot1 = lane == i1
    hot2 = lane == i2
    cnt = jnp.where(hot1 | hot2, 1.0, 0.0)
    row = lax.broadcasted_iota(I32, (tm, tm), 0)
    col = lax.broadcasted_iota(I32, (tm, tm), 1)
    ltri = jnp.where(col < row, 1.0, 0.0).astype(BF16)
    rank = _dot(ltri, cnt.astype(BF16))

    n_row = jnp.sum(cnt, axis=0, keepdims=True)
    chunks = jnp.floor((n_row + float(SUBLANES - 1)) * (1.0 / SUBLANES))
    lrow = lax.broadcasted_iota(I32, (ROUTER_LANES, ROUTER_LANES), 0)
    lcol = lax.broadcasted_iota(I32, (ROUTER_LANES, ROUTER_LANES), 1)
    before = jnp.where(lrow < lcol, 1.0, 0.0).astype(BF16)
    chunks8 = jnp.broadcast_to(chunks, (SUBLANES, ROUTER_LANES))
    so_rows = _dot(chunks8.astype(BF16), before) * float(SUBLANES)
    srow = lax.broadcasted_iota(I32, (SUBLANES, ROUTER_LANES), 0)
    tab = jnp.where(srow == 0, jnp.broadcast_to(n_row, (SUBLANES, ROUTER_LANES)), so_rows)

    place = rank + so_rows[0:1]
    pos1 = jnp.sum(jnp.where(hot1, place, 0.0), axis=-1, keepdims=True)
    pos2 = jnp.sum(jnp.where(hot2, place, 0.0), axis=-1, keepdims=True)
    meta = jnp.where(lane == 0.0, w1, 0.0)
    for j, v in enumerate((w2, pos1, pos2), start=1):
        meta = jnp.where(lane == float(j), v, meta)

    meta_t = jnp.transpose(meta)
    stage_row = lax.broadcasted_iota(I32, (STAGE_ROWS, tm), 0)
    hit = (stage_row == meta_t[2:3, :].astype(I32)) | (stage_row == meta_t[3:4, :].astype(I32))
    sel = jnp.where(hit, 1.0, 0.0).astype(BF16)
    return tab, meta, _pack_pairs(_dot(sel, xh))


def _router_kernel(h_ref, g_ref, wr_ref, br_ref, xs_ref, meta_ref, tab_ref):
    xn = _rms(h_ref[...], g_ref[...])
    xh = xn.astype(BF16)
    xl = (xn - xh.astype(F32)).astype(BF16)
    part = _dot(xh, wr_ref[...])
    logits = part[:, :ROUTER_LANES] + part[:, ROUTER_LANES:] + _dot(xl, wr_ref[:, :ROUTER_LANES])
    biased = logits + br_ref[...]
    for s in range(MOE_STEP_TILES):
        rows = slice(s * TM_MOE, (s + 1) * TM_MOE)
        tab, meta, xs = _route_tile(logits[rows], biased[rows], xh[rows])
        tab_ref[s * SUBLANES:(s + 1) * SUBLANES, :] = tab
        meta_ref[rows, :] = meta
        xs_ref[s * STAGE_ROWS:(s + 1) * STAGE_ROWS, :] = xs


def _router(layer, h, g, wr, br):
    t, d = h.shape
    nt = t // TM_MOE
    k = MOE_STEP_TILES
    return pl.pallas_call(
        _router_kernel,
        grid=(nt // k,),
        in_specs=[pl.BlockSpec((k * TM_MOE, d), lambda i: (i, 0))] + [_layer_spec(a, layer) for a in (g, wr, br)],
        out_specs=[
            pl.BlockSpec((k * STAGE_ROWS, d // 2), lambda i: (i, 0)),
            pl.BlockSpec((k * TM_MOE, ROUTER_LANES), lambda i: (i, 0)),
            pl.BlockSpec((k * SUBLANES, ROUTER_LANES), lambda i: (i, 0)),
        ],
        out_shape=[
            jax.ShapeDtypeStruct((nt * STAGE_ROWS, d // 2), I32),
            jax.ShapeDtypeStruct((t, ROUTER_LANES), F32),
            jax.ShapeDtypeStruct((nt * SUBLANES, ROUTER_LANES), F32),
        ],
        compiler_params=pltpu.CompilerParams(dimension_semantics=("arbitrary",), vmem_limit_bytes=VMEM_LIMIT),
        name="router",
    )(h, g, wr, br)


def _match_runs(query, key, n, val, default):
    d = query - key
    inside = (d >= 0.0) & (d < n)
    hit = jnp.sum(jnp.where(inside, 1.0, 0.0), axis=0, keepdims=True)
    got = jnp.sum(jnp.where(inside, val + d, 0.0), axis=0, keepdims=True)
    return jnp.where(hit > 0.0, got, float(default)).astype(I32)


def _chunk_table_kernel(be_ref, tkey_ref, tn_ref, tval_ref, ekey_ref, en_ref, eval_ref, ctab_ref, ytab_ref):
    lanes = ytab_ref.shape[1]
    nruns = tkey_ref.shape[0]
    lane = lax.broadcasted_iota(I32, (1, lanes), 1).astype(F32)
    tiles_met = -(-lanes // STAGE_CHUNKS) + 1
    for k in range(ytab_ref.shape[0]):
        r0 = min((k * lanes) // STAGE_CHUNKS * N_EXPERTS, nruns - tiles_met * N_EXPERTS)
        rs = slice(r0, r0 + tiles_met * N_EXPERTS)
        ytab_ref[k:k + 1, :] = _match_runs(lane + float(k * lanes), tkey_ref[rs, :], tn_ref[rs, :],
                                           tval_ref[rs, :], 0)

    per_block = ctab_ref.shape[1]
    per_expert = nruns // N_EXPERTS
    lane_b = lane[:, :per_block]

    def block(b, carry):
        r0 = pl.multiple_of(be_ref[b] * per_expert, per_expert)
        rs = pl.ds(r0, per_expert)
        query = lane_b + lax.convert_element_type(b * per_block, F32)
        ctab_ref[pl.ds(b, 1), :] = _match_runs(query, ekey_ref[rs, :per_block], en_ref[rs, :per_block],
                                               eval_ref[rs, :per_block], ZERO_CHUNK)
        return carry

    lax.fori_loop(0, ctab_ref.shape[0], block, 0)


def _chunk_tables(block_e, nch, xs_chunk, run_chunk):
    nt = nch.shape[0]
    nb = block_e.shape[0]
    lanes = ROUTER_LANES
    wide = lambda a: jnp.broadcast_to(a.reshape(-1, 1).astype(F32), (a.size, lanes))
    tile_major = [wide(a) for a in (xs_chunk, nch, run_chunk)]
    expert_major = [wide(a.T) for a in (run_chunk, nch, xs_chunk)]
    full = lambda a: pl.BlockSpec(a.shape, lambda i, be: (0, 0))
    n_stage_rows = nt * STAGE_CHUNKS // lanes
    ctab, ytab = pl.pallas_call(
        _chunk_table_kernel,
        grid_spec=pltpu.PrefetchScalarGridSpec(
            num_scalar_prefetch=1,
            grid=(1,),
            in_specs=[full(a) for a in tile_major + expert_major],
            out_specs=[pl.BlockSpec((nb, BM_EXP // SUBLANES), lambda i, be: (0, 0)),
                       pl.BlockSpec((n_stage_rows, lanes), lambda i, be: (0, 0))],
        ),
        out_shape=[jax.ShapeDtypeStruct((nb, BM_EXP // SUBLANES), I32),
                   jax.ShapeDtypeStruct((n_stage_rows, lanes), I32)],
        name="chunks",
    )(block_e, *tile_major, *expert_major)
    return ctab.reshape(-1), ytab.reshape(-1)


def _expert_kernel(be_ref, nu_ref, plan_ref, tab_ref, xs_hbm, wg_hbm, wu_hbm, wd_hbm, y_ref,
                   xbuf, wgbuf, wubuf, wdbuf, sem, wsem, *, layer):
    b = pl.program_id(0)
    nu = nu_ref[0]
    bm = xbuf.shape[1]
    per_block = bm // SUBLANES

    def start_gather(blk):
        slot = blk % 2
        for c in range(per_block):
            src = pl.multiple_of(tab_ref[blk * per_block + c] * SUBLANES, SUBLANES)
            pltpu.make_async_copy(xs_hbm.at[pl.ds(src, SUBLANES)],
                                  xbuf.at[slot, pl.ds(c * SUBLANES, SUBLANES)], sem.at[slot]).start()

    def wait_gather(blk):
        slot = blk % 2
        pltpu.make_async_copy(xs_hbm.at[pl.ds(0, bm)], xbuf.at[slot], sem.at[slot]).wait()

    def weight_copies(e, slot):
        return [pltpu.make_async_copy(w_hbm.at[layer, e], buf.at[slot], wsem.at[slot])
                for w_hbm, buf in ((wg_hbm, wgbuf), (wu_hbm, wubuf), (wd_hbm, wdbuf))]

    @pl.when(b == 0)
    def _():
        start_gather(0)
        for cp in weight_copies(be_ref[0], plan_ref[0]):
            cp.start()

    @pl.when(b + 1 < nu)
    def _():
        start_gather(b + 1)

    @pl.when(b < nu)
    def _():
        wslot = plan_ref[3 * b]

        @pl.when(plan_ref[3 * b + 1] == 1)
        def _():
            for cp in weight_copies(be_ref[b], wslot):
                cp.wait()
            nxt = plan_ref[3 * b + 2]

            @pl.when(nxt >= 0)
            def _():
                for cp in weight_copies(nxt, 1 - wslot):
                    cp.start()

        wait_gather(b)
        x = _unpack_pairs(xbuf[b % 2])
        a = _dot(x, wgbuf[wslot].astype(BF16))
        u = _dot(x, wubuf[wslot].astype(BF16))
        hmid = (a * _sigmoid(a)) * u
        y = _dot(hmid.astype(BF16), wdbuf[wslot].astype(BF16))
        y_ref[...] = _pack_pairs(y.astype(BF16).astype(F32))

    @pl.when(b >= nu)
    def _():
        y_ref[...] = jnp.zeros_like(y_ref)


def _weight_plan(block_e, n_used):
    nb = block_e.shape[0]
    idx = jnp.arange(nb, dtype=I32)
    used = idx < n_used[0]
    first = used & jnp.concatenate([jnp.ones((1,), bool), block_e[1:] != block_e[:-1]])
    slot = (jnp.cumsum(first.astype(I32)) - 1) % 2
    nxt_idx = lax.cummin(jnp.where(first, idx, nb)[::-1])[::-1]
    nxt_idx = jnp.concatenate([nxt_idx[1:], jnp.full((1,), nb, I32)])
    nxt = jnp.where(nxt_idx < nb, block_e[jnp.minimum(nxt_idx, nb - 1)], -1)
    return jnp.stack([slot, first.astype(I32), nxt], axis=1).reshape(-1).astype(I32)


def _experts(layer, block_e, n_used, tab, xs, wg, wu, wd):
    d, de = wg.shape[-2:]
    dp = xs.shape[1]
    nb = block_e.shape[0]
    hbm = pl.BlockSpec(memory_space=pl.ANY)
    return pl.pallas_call(
        functools.partial(_expert_kernel, layer=layer),
        grid_spec=pltpu.PrefetchScalarGridSpec(
            num_scalar_prefetch=4,
            grid=(nb,),
            in_specs=[hbm, hbm, hbm, hbm],
            out_specs=pl.BlockSpec((BM_EXP, dp), lambda b, *s: (b, 0)),
            scratch_shapes=[
                pltpu.VMEM((2, BM_EXP, dp), I32),
                pltpu.VMEM((2, d, de), F32), pltpu.VMEM((2, d, de), F32), pltpu.VMEM((2, de, d), F32),
                pltpu.SemaphoreType.DMA((2,)), pltpu.SemaphoreType.DMA((2,)),
            ],
        ),
        out_shape=jax.ShapeDtypeStruct((nb * BM_EXP, dp), I32),
        compiler_params=pltpu.CompilerParams(dimension_semantics=("arbitrary",), vmem_limit_bytes=VMEM_LIMIT),
        name="experts",
    )(block_e, n_used, _weight_plan(block_e, n_used), tab, xs, wg, wu, wd)


def _combine_kernel(ytab_ref, h_ref, meta_ref, p_ref, wpe_ref, npe_ref, npg_ref,
                    wpg_ref, nf_ref, y_hbm, o_ref, stage, sem, *, final):
    i = pl.program_id(0)

    step_chunks = MOE_STEP_TILES * STAGE_CHUNKS

    def start_gather(step, slot):
        for c in range(step_chunks):
            src = pl.multiple_of(ytab_ref[step * step_chunks + c] * SUBLANES, SUBLANES)
            pltpu.make_async_copy(y_hbm.at[pl.ds(src, SUBLANES)],
                                  stage.at[slot, pl.ds(c * SUBLANES, SUBLANES)], sem.at[slot]).start()

    def wait_gather(slot):
        pltpu.make_async_copy(y_hbm.at[pl.ds(0, stage.shape[1])], stage.at[slot], sem.at[slot]).wait()

    @pl.when(i == 0)
    def _():
        start_gather(0, 0)

    last = i == pl.num_programs(0) - 1
    start_gather(jnp.where(last, 0, i + 1), (i + 1) % 2)

    emb = _rms(_dot(p_ref[...].astype(BF16), wpe_ref[...]), npe_ref[...])

    slot = i % 2
    wait_gather(slot)

    moe = []
    for s in range(MOE_STEP_TILES):
        meta = meta_ref[s * TM_MOE:(s + 1) * TM_MOE, :]
        s1, s2 = _stage_selectors(meta)
        sel = (jnp.where(s1, meta[:, 0:1], 0.0) + jnp.where(s2, meta[:, 1:2], 0.0)).astype(BF16)
        moe.append(_dot(sel, _unpack_pairs(stage[slot, s * STAGE_ROWS:(s + 1) * STAGE_ROWS, :])))
    h2 = h_ref[...] + jnp.concatenate(moe, axis=0)
    gate = _sigmoid(_dot(_rms(h2, npg_ref[...]).astype(BF16), wpg_ref[...]))
    out = h2 + gate * emb
    if final:
        out = _rms(out, nf_ref[...])
    o_ref[...] = out

    @pl.when(last)
    def _():
        wait_gather((i + 1) % 2)


def _combine(layer, ytab, h, meta, p, wpe, npe, npg, wpg, nf, y, final):
    t, d = h.shape
    k = MOE_STEP_TILES
    tm = k * TM_MOE
    tile = lambda r, w: pl.BlockSpec((r, w), lambda i, *s: (i, 0))
    return pl.pallas_call(
        functools.partial(_combine_kernel, final=final),
        grid_spec=pltpu.PrefetchScalarGridSpec(
            num_scalar_prefetch=1,
            grid=(t // tm,),
            in_specs=[
                tile(tm, d), tile(tm, ROUTER_LANES),
                pl.BlockSpec((None, None, tm, p.shape[-1]), lambda i, *s: (layer, 0, i, 0)),
                _layer_spec(wpe, layer), _layer_spec(npe, layer), _layer_spec(npg, layer), _layer_spec(wpg, layer),
                pl.BlockSpec(nf.shape, lambda i, *s: (0, 0)),
                pl.BlockSpec(memory_space=pl.ANY),
            ],
            out_specs=tile(tm, d),
            scratch_shapes=[pltpu.VMEM((2, k * STAGE_ROWS, y.shape[1]), I32), pltpu.SemaphoreType.DMA((2,))],
        ),
        out_shape=jax.ShapeDtypeStruct((t, d), F32),
        compiler_params=pltpu.CompilerParams(dimension_semantics=("arbitrary",), vmem_limit_bytes=VMEM_LIMIT),
        name="combine",
    )(ytab, h, meta, p, wpe, npe, npg, wpg, nf, y)


def _routing_tables(tab, t):
    nt = t // TM_MOE
    tab = tab.reshape(nt, SUBLANES, ROUTER_LANES)[:, :, EXPERT_LANE0:EXPERT_LANE0 + N_EXPERTS]
    counts = tab[:, 0].astype(I32)
    stage_chunk = tab[:, 1].astype(I32) // SUBLANES
    nch = (counts + SUBLANES - 1) // SUBLANES
    per_block = BM_EXP // SUBLANES
    seg = jnp.sum(nch, axis=0)
    padded = (seg + per_block - 1) // per_block * per_block
    pend = jnp.cumsum(padded)
    pstart = pend - padded
    run_chunk = pstart[None, :] + jnp.cumsum(nch, axis=0) - nch
    nb = -(-(TOP_K * t + (SUBLANES - 1) * nt * N_EXPERTS) // BM_EXP) + N_EXPERTS
    bstart = jnp.arange(nb, dtype=I32) * per_block
    block_e = jnp.minimum(jnp.sum((pend[None, :] <= bstart[:, None]).astype(I32), axis=1), N_EXPERTS - 1)
    n_used = pend[-1:] // per_block
    tile_chunk0 = jnp.arange(nt, dtype=I32)[:, None] * STAGE_CHUNKS
    return nch, tile_chunk0 + stage_chunk, run_chunk, block_e.astype(I32), n_used.astype(I32)


def kernel(x, p, norm_mix, w_in, lb_logits, rec_out_gain, sgu_ln_g, sgu_ln_b, sgu_w, sgu_b, w_out, norm_ffn, w_rg, b_rg, w_re, b_re, w_gate, w_up, w_down, norm_pg, w_pg, w_pe, norm_pe, norm_f):
    bsz, seq, d = x.shape
    depth = w_in.shape[0]
    assert bsz == 1 and seq % (MOE_STEP_TILES * TM_MOE) == 0 and seq % TM_MIX == 0
    h = x.reshape(seq, d)

    lb = jnp.cumsum(jax.nn.softmax(lb_logits.astype(F32), axis=0), axis=0)
    lb = lb - lb[0]
    rows = lambda v: v.reshape(depth, 1, -1)
    la, l1, oml = rows(jnp.log(lb)), rows(jnp.log1p(-lb)), rows(1.0 - lb)
    gain = rows(jnp.tile(rec_out_gain, (1, H_REC)))
    sb = jnp.repeat(jnp.swapaxes(sgu_b, 1, 2), D_SGU // H_SGU, axis=2)
    win_b, wout_b, wpg_b, wpe_b = (w.astype(BF16) for w in (w_in, w_out, w_pg, w_pe))
    npad = ROUTER_LANES - N_GROUPS - N_EXPERTS
    wr = jnp.concatenate([w_rg, w_re, jnp.zeros((depth, d, npad), F32)], axis=2)
    wr_hi = wr.astype(BF16)
    wr = jnp.concatenate([wr_hi, (wr - wr_hi.astype(F32)).astype(BF16)], axis=2)
    br = rows(jnp.concatenate([b_rg, b_re, jnp.zeros((depth, npad), F32)], axis=1))
    wsel = jnp.asarray(np.tile(_decay_selectors(), (1, 3)), BF16)
    lev = jnp.asarray(_pair_levels())
    gmix, gffn, gpg, gpe, lng, lnb = (rows(v) for v in (norm_mix, norm_ffn, norm_pg, norm_pe, sgu_ln_g, sgu_ln_b))
    nf = norm_f.reshape(1, d)

    for l in range(depth):
        h = _mixer(l, h, gmix, win_b, la, l1, oml, gain, lng, lnb, sgu_w, sb, wout_b, wsel, lev)
        xs, meta, tab = _router(l, h, gffn, wr, br)
        nch, xs_chunk, run_chunk, block_e, n_used = _routing_tables(tab, seq)
        ctab, ytab = _chunk_tables(block_e, nch, xs_chunk, run_chunk)
        y = _experts(l, block_e, n_used, ctab, xs, w_gate, w_up, w_down)
        h = _combine(l, ytab, h, meta, p, wpe_b, gpe, gpg, wpg_b, nf, y, final=(l == depth - 1))
    return h.reshape(bsz, seq, d)
```

```python
import functools

import numpy as np
import jax
import jax.numpy as jnp
from jax import lax
from jax.experimental import pallas as pl
from jax.experimental.pallas import tpu as pltpu

F32 = jnp.float32
BF16 = jnp.bfloat16
I32 = jnp.int32

EPS = 1e-6
D_REC = 512
D_SGU = 512
H_REC = 4
DK_REC = 128
CHUNK_REC = 64
SGU_CHUNK = 128
H_SGU = 4
N_GROUPS = 4
EXP_PER_GROUP = 8
N_EXPERTS = N_GROUPS * EXP_PER_GROUP
TOP_K = 2
ROUTER_LANES = 128
EXPERT_LANE0 = N_GROUPS
SUBLANES = 8

TM_MIX = 512
TM_MOE = 256
MOE_STEP_TILES = 4
STAGE_ROWS = 768
BM_EXP = 512
DMA_PRIORITIES = 2
WEIGHT_DMA_PRIORITY = 1
VMEM_LIMIT = 48 * 1024 * 1024

assert STAGE_ROWS >= TOP_K * TM_MOE + (SUBLANES - 1) * N_EXPERTS + SUBLANES
STAGE_CHUNKS = STAGE_ROWS // SUBLANES
ZERO_CHUNK = STAGE_CHUNKS - 1
HIGH_HALF = -65536

_LEVELS = (2, 4, 8, 16, 32, 64)


def _decay_selectors():
    t = np.arange(CHUNK_REC)[:, None]
    s = np.arange(CHUNK_REC)[None, :]
    mats = [((t // m == s // m) & (s <= t)) for m in _LEVELS]
    mats += [((t // m == s // m) & (s > t)) for m in _LEVELS if m < SUBLANES]
    return np.concatenate(mats, axis=0).astype(np.float32)


def _pair_levels():
    t = np.arange(CHUNK_REC)[:, None]
    s = np.arange(CHUNK_REC)[None, :]
    x = t ^ s
    lev = np.floor(np.log2(np.maximum(x, 1))).astype(np.int32)
    lev = np.where(t == s, -1, lev)
    lev = np.where(s > t, -2, lev)
    return lev.astype(np.int32)


def _rms(x, g):
    return x * lax.rsqrt(jnp.mean(x * x, axis=-1, keepdims=True) + EPS) * g


def _sigmoid(x):
    return 0.5 * jnp.tanh(0.5 * x) + 0.5


def _gelu_tanh(x):
    return x * (0.5 * (1.0 + jnp.tanh(0.7978845608028654 * (x + 0.044715 * (x * x * x)))))


def _dot(a, b):
    return jnp.dot(a, b, preferred_element_type=F32)


def _dot_nt(a, b):
    return lax.dot_general(a, b, (((1,), (1,)), ((), ())), preferred_element_type=F32)


def _dot_tn(a, b):
    return lax.dot_general(a, b, (((0,), (0,)), ((), ())), preferred_element_type=F32)


def _pack_pairs(x):
    bits = lax.bitcast_convert_type(x, I32)
    half = x.shape[1] // 2
    return lax.shift_right_logical(bits[:, :half], 16) | (bits[:, half:] & HIGH_HALF)


def _unpack_pairs(w):
    lo = lax.bitcast_convert_type(lax.shift_left(w, 16), F32)
    hi = lax.bitcast_convert_type(w & HIGH_HALF, F32)
    return jnp.concatenate([lo, hi], axis=1).astype(BF16)


def _layer_spec(arr, layer):
    zeros = (0,) * (arr.ndim - 1)
    return pl.BlockSpec((None,) + arr.shape[1:], lambda *a: (layer,) + zeros)


def _mixer_kernel(h_ref, gmix_ref, win_ref, la_ref, l1_ref, oml_ref, gain_ref, lng_ref, lnb_ref,
                  ws_ref, sb_ref, wout_ref, wsel_ref, lev_ref, o_ref, st_ref, ocat_ref):
    tm = h_ref.shape[0]

    @pl.when(pl.program_id(0) == 0)
    def _():
        st_ref[...] = jnp.zeros_like(st_ref)

    h = h_ref[...]
    hb = _rms(h, gmix_ref[...]).astype(BF16)

    def proj(j):
        return _dot(hb, win_ref[:, j * 512:(j + 1) * 512])

    zq = proj(0)
    q = zq * _sigmoid(zq)
    fz = proj(1)
    iv = proj(2).astype(BF16)
    zg = proj(3)
    gate = zg * _sigmoid(zg)

    ls = jnp.minimum(fz, 0.0) - jnp.log1p(jnp.exp(-jnp.abs(fz)))
    c = l1_ref[...] + ls
    la = la_ref[...]
    lf = jnp.maximum(la, c) + jnp.log1p(jnp.exp(-jnp.abs(la - c)))
    kk = oml_ref[...] * _sigmoid(-fz)

    wsel = wsel_ref[...]
    lev = lev_ref[...]
    gain = gain_ref[...]
    nlev = len(_LEVELS)
    states = [st_ref[hh] for hh in range(H_REC)]

    zu = proj(4)
    zv = proj(5)
    r128 = lax.broadcasted_iota(I32, (SGU_CHUNK, SGU_CHUNK), 0)
    c128 = lax.broadcasted_iota(I32, (SGU_CHUNK, SGU_CHUNK), 1)
    wtri = [jnp.where(c128 <= r128, ws_ref[hh], 0.0).astype(BF16) for hh in range(H_SGU)]
    sgu_act = {}

    def sgu_activations(c2):
        sl2 = slice(c2 * SGU_CHUNK, (c2 + 1) * SGU_CHUNK)
        ua = _gelu_tanh(zu[sl2])
        va = _gelu_tanh(zv[sl2])
        mu = jnp.mean(va, axis=-1, keepdims=True)
        vc = va - mu
        var = jnp.mean(vc * vc, axis=-1, keepdims=True)
        sgu_act[c2] = (ua, (vc * lax.rsqrt(var + EPS) * lng_ref[...] + lnb_ref[...]).astype(BF16))

    def sgu_mix(c2):
        sl2 = slice(c2 * SGU_CHUNK, (c2 + 1) * SGU_CHUNK)
        ua, vn = sgu_act.pop(c2)
        for hh in range(H_SGU):
            hs = slice(hh * 128, (hh + 1) * 128)
            s = _dot(wtri[hh], vn[:, hs]) + sb_ref[:, hs]
            ocat_ref[sl2, D_REC + hh * 128:D_REC + (hh + 1) * 128] = (ua[:, hs] * s).astype(BF16)

    rec_per_sgu = SGU_CHUNK // CHUNK_REC

    for cidx in range(tm // CHUNK_REC):
        sl = slice(cidx * CHUNK_REC, (cidx + 1) * CHUNK_REC)
        lf_c = lf[sl]
        hi = lf_c.astype(BF16)
        rem = lf_c - hi.astype(F32)
        mid = rem.astype(BF16)
        lo = (rem - mid.astype(F32)).astype(BF16)
        lsum = _dot(wsel, jnp.concatenate([hi, mid, lo], axis=0))
        pd = jnp.exp(lsum[:nlev * 64])
        pe_small = jnp.exp(lsum[nlev * 64:])
        pe = []
        for j, m in enumerate(_LEVELS):
            if m < SUBLANES:
                pe.append(pe_small[j * 64:(j + 1) * 64])
            else:
                blk = lsum[j * 64:(j + 1) * 64].reshape(CHUNK_REC // m, m, D_REC)
                pe.append(jnp.exp((blk[:, m - 1:m, :] - blk).reshape(CHUNK_REC, D_REC)))
        f_c = jnp.exp(lf_c)
        for hh in range(H_REC):
            hs = slice(hh * DK_REC, (hh + 1) * DK_REC)
            q_h = q[sl, hs]
            k_h = kk[sl, hs]
            i_h = iv[sl, hs]
            k_b = k_h.astype(BF16)
            sc = jnp.where(lev == -1, _dot_nt(q_h.astype(BF16), k_b), 0.0)
            sc = sc + jnp.where(lev == 0, _dot_nt((q_h * f_c[:, hs]).astype(BF16), k_b), 0.0)
            for j in range(nlev - 1):
                qd = (q_h * pd[j * 64:(j + 1) * 64, hs]).astype(BF16)
                kd = (k_h * pe[j][:, hs]).astype(BF16)
                sc = sc + jnp.where(lev == j + 1, _dot_nt(qd, kd), 0.0)
            jl = nlev - 1
            qb = (q_h * pd[jl * 64:(jl + 1) * 64, hs]).astype(BF16)
            kd = (k_h * pe[jl][:, hs]).astype(BF16)
            st = states[hh]
            o = _dot(sc.astype(BF16), i_h) + _dot_nt(qb, st.astype(BF16))
            dec = pd[jl * 64 + 63:jl * 64 + 64, hs]
            states[hh] = st * dec + _dot_tn(i_h, kd)
            o = o * lax.rsqrt(jnp.mean(o * o, axis=-1, keepdims=True) + EPS) * gain[:, hs]
            ocat_ref[sl, hs] = (o * gate[sl, hs]).astype(BF16)
        if cidx % rec_per_sgu == 0:
            sgu_activations(cidx // rec_per_sgu)
        if cidx % rec_per_sgu == rec_per_sgu - 1:
            sgu_mix(cidx // rec_per_sgu)

    for hh in range(H_REC):
        st_ref[hh] = states[hh]

    o_ref[...] = h + _dot(ocat_ref[...], wout_ref[...])


def _mixer(layer, h, gmix, win, la, l1, oml, gain, lng, lnb, ws, sb, wout, wsel, lev):
    t, d = h.shape
    tm = min(TM_MIX, t)
    per_layer = (gmix, win, la, l1, oml, gain, lng, lnb, ws, sb, wout)
    return pl.pallas_call(
        _mixer_kernel,
        grid=(t // tm,),
        in_specs=[pl.BlockSpec((tm, d), lambda i: (i, 0))]
        + [_layer_spec(a, layer) for a in per_layer]
        + [pl.BlockSpec(wsel.shape, lambda i: (0, 0)), pl.BlockSpec(lev.shape, lambda i: (0, 0))],
        out_specs=pl.BlockSpec((tm, d), lambda i: (i, 0)),
        out_shape=jax.ShapeDtypeStruct((t, d), F32),
        scratch_shapes=[pltpu.VMEM((H_REC, DK_REC, DK_REC), F32), pltpu.VMEM((tm, d), BF16)],
        compiler_params=pltpu.CompilerParams(dimension_semantics=("arbitrary",), vmem_limit_bytes=VMEM_LIMIT),
        name="mixer",
    )(h, *per_layer, wsel, lev)


def _stage_selectors(meta):
    col = lax.broadcasted_iota(I32, (meta.shape[0], STAGE_ROWS), 1)
    return [col == meta[:, 2 + k:3 + k].astype(I32) for k in range(TOP_K)]


def _route_tile(logits, biased, xh):
    tm = logits.shape[0]
    lane = lax.broadcasted_iota(I32, (tm, ROUTER_LANES), 1).astype(F32)
    ninf = -jnp.inf
    big = float(ROUTER_LANES)

    def first_argmax(v):
        m = jnp.max(v, axis=-1, keepdims=True)
        return jnp.min(jnp.where(v == m, lane, big), axis=-1, keepdims=True)

    def pick(v, idx):
        return jnp.sum(jnp.where(lane == idx, v, 0.0), axis=-1, keepdims=True)

    is_g = lane < float(N_GROUPS)
    g_sel = first_argmax(jnp.where(is_g, biased, ninf))
    lg = jnp.where(is_g, logits, ninf)
    eg = jnp.exp(lg - jnp.max(lg, axis=-1, keepdims=True))
    g_w = pick(eg, g_sel) / jnp.sum(eg, axis=-1, keepdims=True)

    lo = float(EXPERT_LANE0) + float(EXP_PER_GROUP) * g_sel
    is_e = (lane >= lo) & (lane < lo + float(EXP_PER_GROUP))
    eb = jnp.where(is_e, biased, ninf)
    i1 = first_argmax(eb)
    i2 = first_argmax(jnp.where(lane == i1, ninf, eb))
    le = jnp.where(is_e, logits, ninf)
    ee = jnp.exp(le - jnp.max(le, axis=-1, keepdims=True))
    p1 = pick(ee, i1)
    p2 = pick(ee, i2)
    w1 = g_w * (p1 / (p1 + p2))
    w2 = g_w * (p2 / (p1 + p2))

    hot1 = lane == i1
    hot2 = lane == i2
    cnt = jnp.where(hot1 | hot2, 1.0, 0.0)
    row = lax.broadcasted_iota(I32, (tm, tm), 0)
    col = lax.broadcasted_iota(I32, (tm, tm), 1)
    ltri = jnp.where(col < row, 1.0, 0.0).astype(BF16)
    rank = _dot(ltri, cnt.astype(BF16))

    n_row = jnp.sum(cnt, axis=0, keepdims=True)
    chunks = jnp.floor((n_row + float(SUBLANES - 1)) * (1.0 / SUBLANES))
    lrow = lax.broadcasted_iota(I32, (ROUTER_LANES, ROUTER_LANES), 0)
    lcol = lax.broadcasted_iota(I32, (ROUTER_LANES, ROUTER_LANES), 1)
    before = jnp.where(lrow < lcol, 1.0, 0.0).astype(BF16)
    chunks8 = jnp.broadcast_to(chunks, (SUBLANES, ROUTER_LANES))
    so_rows = _dot(chunks8.astype(BF16), before) * float(SUBLANES)
    srow = lax.broadcasted_iota(I32, (SUBLANES, ROUTER_LANES), 0)
    tab = jnp.where(srow == 0, jnp.broadcast_to(n_row, (SUBLANES, ROUTER_LANES)), so_rows)

    place = rank + so_rows[0:1]
    pos1 = jnp.sum(jnp.where(hot1, place, 0.0), axis=-1, keepdims=True)
    pos2 = jnp.sum(jnp.where(hot2, place, 0.0), axis=-1, keepdims=True)
    meta = jnp.where(lane == 0.0, w1, 0.0)
    for j, v in enumerate((w2, pos1, pos2), start=1):
        meta = jnp.where(lane == float(j), v, meta)

    meta_t = jnp.transpose(meta)
    stage_row = lax.broadcasted_iota(I32, (STAGE_ROWS, tm), 0)
    hit = (stage_row == meta_t[2:3, :].astype(I32)) | (stage_row == meta_t[3:4, :].astype(I32))
    sel = jnp.where(hit, 1.0, 0.0).astype(BF16)
    return tab, meta, _pack_pairs(_dot(sel, xh))


def _router_kernel(h_ref, g_ref, wr_ref, br_ref, xs_ref, meta_ref, tab_ref):
    xn = _rms(h_ref[...], g_ref[...])
    xh = xn.astype(BF16)
    xl = (xn - xh.astype(F32)).astype(BF16)
    part = _dot(xh, wr_ref[...])
    logits = part[:, :ROUTER_LANES] + part[:, ROUTER_LANES:] + _dot(xl, wr_ref[:, :ROUTER_LANES])
    biased = logits + br_ref[...]
    for s in range(MOE_STEP_TILES):
        rows = slice(s * TM_MOE, (s + 1) * TM_MOE)
        tab, meta, xs = _route_tile(logits[rows], biased[rows], xh[rows])
        tab_ref[s * SUBLANES:(s + 1) * SUBLANES, :] = tab
        meta_ref[rows, :] = meta
        xs_ref[s * STAGE_ROWS:(s + 1) * STAGE_ROWS, :] = xs


def _router(layer, h, g, wr, br):
    t, d = h.shape
    nt = t // TM_MOE
    k = MOE_STEP_TILES
    return pl.pallas_call(
        _router_kernel,
        grid=(nt // k,),
        in_specs=[pl.BlockSpec((k * TM_MOE, d), lambda i: (i, 0))] + [_layer_spec(a, layer) for a in (g, wr, br)],
        out_specs=[
            pl.BlockSpec((k * STAGE_ROWS, d // 2), lambda i: (i, 0)),
            pl.BlockSpec((k * TM_MOE, ROUTER_LANES), lambda i: (i, 0)),
            pl.BlockSpec((k * SUBLANES, ROUTER_LANES), lambda i: (i, 0)),
        ],
        out_shape=[
            jax.ShapeDtypeStruct((nt * STAGE_ROWS, d // 2), I32),
            jax.ShapeDtypeStruct((t, ROUTER_LANES), F32),
            jax.ShapeDtypeStruct((nt * SUBLANES, ROUTER_LANES), F32),
        ],
        compiler_params=pltpu.CompilerParams(dimension_semantics=("arbitrary",), vmem_limit_bytes=VMEM_LIMIT),
        name="router",
    )(h, g, wr, br)


def _match_runs(query, key, n, val, default):
    d = query - key
    inside = (d >= 0.0) & (d < n)
    hit = jnp.sum(jnp.where(inside, 1.0, 0.0), axis=0, keepdims=True)
    got = jnp.sum(jnp.where(inside, val + d, 0.0), axis=0, keepdims=True)
    return jnp.where(hit > 0.0, got, float(default)).astype(I32)


def _chunk_table_kernel(be_ref, tkey_ref, tn_ref, tval_ref, ekey_ref, en_ref, eval_ref, ctab_ref, ytab_ref):
    lanes = ytab_ref.shape[1]
    nruns = tkey_ref.shape[0]
    lane = lax.broadcasted_iota(I32, (1, lanes), 1).astype(F32)
    tiles_met = -(-lanes // STAGE_CHUNKS) + 1
    for k in range(ytab_ref.shape[0]):
        r0 = min((k * lanes) // STAGE_CHUNKS * N_EXPERTS, nruns - tiles_met * N_EXPERTS)
        rs = slice(r0, r0 + tiles_met * N_EXPERTS)
        ytab_ref[k:k + 1, :] = _match_runs(lane + float(k * lanes), tkey_ref[rs, :], tn_ref[rs, :],
                                           tval_ref[rs, :], 0)

    per_block = ctab_ref.shape[1]
    per_expert = nruns // N_EXPERTS
    lane_b = lane[:, :per_block]

    def block(b, carry):
        r0 = pl.multiple_of(be_ref[b] * per_expert, per_expert)
        rs = pl.ds(r0, per_expert)
        query = lane_b + lax.convert_element_type(b * per_block, F32)
        ctab_ref[pl.ds(b, 1), :] = _match_runs(query, ekey_ref[rs, :per_block], en_ref[rs, :per_block],
                                               eval_ref[rs, :per_block], ZERO_CHUNK)
        return carry

    lax.fori_loop(0, ctab_ref.shape[0], block, 0)


def _chunk_tables(block_e, nch, xs_chunk, run_chunk):
    nt = nch.shape[0]
    nb = block_e.shape[0]
    lanes = ROUTER_LANES
    wide = lambda a: jnp.broadcast_to(a.reshape(-1, 1).astype(F32), (a.size, lanes))
    tile_major = [wide(a) for a in (xs_chunk, nch, run_chunk)]
    expert_major = [wide(a.T) for a in (run_chunk, nch, xs_chunk)]
    full = lambda a: pl.BlockSpec(a.shape, lambda i, be: (0, 0))
    n_stage_rows = nt * STAGE_CHUNKS // lanes
    ctab, ytab = pl.pallas_call(
        _chunk_table_kernel,
        grid_spec=pltpu.PrefetchScalarGridSpec(
            num_scalar_prefetch=1,
            grid=(1,),
            in_specs=[full(a) for a in tile_major + expert_major],
            out_specs=[pl.BlockSpec((nb, BM_EXP // SUBLANES), lambda i, be: (0, 0)),
                       pl.BlockSpec((n_stage_rows, lanes), lambda i, be: (0, 0))],
        ),
        out_shape=[jax.ShapeDtypeStruct((nb, BM_EXP // SUBLANES), I32),
                   jax.ShapeDtypeStruct((n_stage_rows, lanes), I32)],
        name="chunks",
    )(block_e, *tile_major, *expert_major)
    return ctab.reshape(-1), ytab.reshape(-1)


def _expert_kernel(be_ref, nu_ref, plan_ref, tab_ref, xs_hbm, wg_hbm, wu_hbm, wd_hbm, y_ref,
                   xbuf, wgbuf, wubuf, wdbuf, sem, wsem, *, layer):
    b = pl.program_id(0)
    nu = nu_ref[0]
    bm = xbuf.shape[1]
    per_block = bm // SUBLANES

    def start_gather(blk):
        slot = blk % 2
        for c in range(per_block):
            src = pl.multiple_of(tab_ref[blk * per_block + c] * SUBLANES, SUBLANES)
            pltpu.make_async_copy(xs_hbm.at[pl.ds(src, SUBLANES)],
                                  xbuf.at[slot, pl.ds(c * SUBLANES, SUBLANES)], sem.at[slot]).start()

    def wait_gather(blk):
        slot = blk % 2
        pltpu.make_async_copy(xs_hbm.at[pl.ds(0, bm)], xbuf.at[slot], sem.at[slot]).wait()

    def weight_copies(e, slot):
        return [pltpu.make_async_copy(w_hbm.at[layer, e], buf.at[slot], wsem.at[slot])
                for w_hbm, buf in ((wg_hbm, wgbuf), (wu_hbm, wubuf), (wd_hbm, wdbuf))]

    @pl.when(b == 0)
    def _():
        start_gather(0)
        for cp in weight_copies(be_ref[0], plan_ref[0]):
            cp.start(priority=WEIGHT_DMA_PRIORITY)

    @pl.when(b + 1 < nu)
    def _():
        start_gather(b + 1)

    @pl.when(b < nu)
    def _():
        wslot = plan_ref[3 * b]

        @pl.when(plan_ref[3 * b + 1] == 1)
        def _():
            for cp in weight_copies(be_ref[b], wslot):
                cp.wait()
            nxt = plan_ref[3 * b + 2]

            @pl.when(nxt >= 0)
            def _():
                for cp in weight_copies(nxt, 1 - wslot):
                    cp.start(priority=WEIGHT_DMA_PRIORITY)

        wait_gather(b)
        x = _unpack_pairs(xbuf[b % 2])
        a = _dot(x, wgbuf[wslot].astype(BF16))
        u = _dot(x, wubuf[wslot].astype(BF16))
        hmid = (a * _sigmoid(a)) * u
        y = _dot(hmid.astype(BF16), wdbuf[wslot].astype(BF16))
        y_ref[...] = _pack_pairs(y.astype(BF16).astype(F32))

    @pl.when(b >= nu)
    def _():
        y_ref[...] = jnp.zeros_like(y_ref)


def _weight_plan(block_e, n_used):
    nb = block_e.shape[0]
    idx = jnp.arange(nb, dtype=I32)
    used = idx < n_used[0]
    first = used & jnp.concatenate([jnp.ones((1,), bool), block_e[1:] != block_e[:-1]])
    slot = (jnp.cumsum(first.astype(I32)) - 1) % 2
    nxt_idx = lax.cummin(jnp.where(first, idx, nb)[::-1])[::-1]
    nxt_idx = jnp.concatenate([nxt_idx[1:], jnp.full((1,), nb, I32)])
    nxt = jnp.where(nxt_idx < nb, block_e[jnp.minimum(nxt_idx, nb - 1)], -1)
    return jnp.stack([slot, first.astype(I32), nxt], axis=1).reshape(-1).astype(I32)


def _experts(layer, block_e, n_used, tab, xs, wg, wu, wd):
    d, de = wg.shape[-2:]
    dp = xs.shape[1]
    nb = block_e.shape[0]
    hbm = pl.BlockSpec(memory_space=pl.ANY)
    return pl.pallas_call(
        functools.partial(_expert_kernel, layer=layer),
        grid_spec=pltpu.PrefetchScalarGridSpec(
            num_scalar_prefetch=4,
            grid=(nb,),
            in_specs=[hbm, hbm, hbm, hbm],
            out_specs=pl.BlockSpec((BM_EXP, dp), lambda b, *s: (b, 0)),
            scratch_shapes=[
                pltpu.VMEM((2, BM_EXP, dp), I32),
                pltpu.VMEM((2, d, de), F32), pltpu.VMEM((2, d, de), F32), pltpu.VMEM((2, de, d), F32),
                pltpu.SemaphoreType.DMA((2,)), pltpu.SemaphoreType.DMA((2,)),
            ],
        ),
        out_shape=jax.ShapeDtypeStruct((nb * BM_EXP, dp), I32),
        compiler_params=pltpu.CompilerParams(dimension_semantics=("arbitrary",), vmem_limit_bytes=VMEM_LIMIT),
        name="experts",
    )(block_e, n_used, _weight_plan(block_e, n_used), tab, xs, wg, wu, wd)


def _combine_kernel(ytab_ref, h_ref, meta_ref, p_ref, wpe_ref, npe_ref, npg_ref,
                    wpg_ref, nf_ref, y_hbm, o_ref, stage, sem, *, final):
    i = pl.program_id(0)

    step_chunks = MOE_STEP_TILES * STAGE_CHUNKS

    def start_gather(step):
        slot = step % 2
        for c in range(step_chunks):
            src = pl.multiple_of(ytab_ref[step * step_chunks + c] * SUBLANES, SUBLANES)
            pltpu.make_async_copy(y_hbm.at[pl.ds(src, SUBLANES)], stage.at[slot, pl.ds(c * SUBLANES, SUBLANES)],
                                  sem.at[slot]).start(priority=c % DMA_PRIORITIES)

    @pl.when(i == 0)
    def _():
        start_gather(0)

    @pl.when(i + 1 < pl.num_programs(0))
    def _():
        start_gather(i + 1)

    emb = _rms(_dot(p_ref[...].astype(BF16), wpe_ref[...]), npe_ref[...])

    slot = i % 2
    pltpu.make_async_copy(y_hbm.at[pl.ds(0, stage.shape[1])], stage.at[slot], sem.at[slot]).wait()

    moe = []
    for s in range(MOE_STEP_TILES):
        meta = meta_ref[s * TM_MOE:(s + 1) * TM_MOE, :]
        s1, s2 = _stage_selectors(meta)
        sel = (jnp.where(s1, meta[:, 0:1], 0.0) + jnp.where(s2, meta[:, 1:2], 0.0)).astype(BF16)
        moe.append(_dot(sel, _unpack_pairs(stage[slot, s * STAGE_ROWS:(s + 1) * STAGE_ROWS, :])))
    h2 = h_ref[...] + jnp.concatenate(moe, axis=0)
    gate = _sigmoid(_dot(_rms(h2, npg_ref[...]).astype(BF16), wpg_ref[...]))
    out = h2 + gate * emb
    if final:
        out = _rms(out, nf_ref[...])
    o_ref[...] = out


def _combine(layer, ytab, h, meta, p, wpe, npe, npg, wpg, nf, y, final):
    t, d = h.shape
    k = MOE_STEP_TILES
    tm = k * TM_MOE
    tile = lambda r, w: pl.BlockSpec((r, w), lambda i, *s: (i, 0))
    return pl.pallas_call(
        functools.partial(_combine_kernel, final=final),
        grid_spec=pltpu.PrefetchScalarGridSpec(
            num_scalar_prefetch=1,
            grid=(t // tm,),
            in_specs=[
                tile(tm, d), tile(tm, ROUTER_LANES),
                pl.BlockSpec((None, None, tm, p.shape[-1]), lambda i, *s: (layer, 0, i, 0)),
                _layer_spec(wpe, layer), _layer_spec(npe, layer), _layer_spec(npg, layer), _layer_spec(wpg, layer),
                pl.BlockSpec(nf.shape, lambda i, *s: (0, 0)),
                pl.BlockSpec(memory_space=pl.ANY),
            ],
            out_specs=tile(tm, d),
            scratch_shapes=[pltpu.VMEM((2, k * STAGE_ROWS, y.shape[1]), I32), pltpu.SemaphoreType.DMA((2,))],
        ),
        out_shape=jax.ShapeDtypeStruct((t, d), F32),
        compiler_params=pltpu.CompilerParams(dimension_semantics=("arbitrary",), vmem_limit_bytes=VMEM_LIMIT),
        name="combine",
    )(ytab, h, meta, p, wpe, npe, npg, wpg, nf, y)


def _routing_tables(tab, t):
    nt = t // TM_MOE
    tab = tab.reshape(nt, SUBLANES, ROUTER_LANES)[:, :, EXPERT_LANE0:EXPERT_LANE0 + N_EXPERTS]
    counts = tab[:, 0].astype(I32)
    stage_chunk = tab[:, 1].astype(I32) // SUBLANES
    nch = (counts + SUBLANES - 1) // SUBLANES
    per_block = BM_EXP // SUBLANES
    seg = jnp.sum(nch, axis=0)
    padded = (seg + per_block - 1) // per_block * per_block
    pend = jnp.cumsum(padded)
    pstart = pend - padded
    run_chunk = pstart[None, :] + jnp.cumsum(nch, axis=0) - nch
    nb = -(-(TOP_K * t + (SUBLANES - 1) * nt * N_EXPERTS) // BM_EXP) + N_EXPERTS
    bstart = jnp.arange(nb, dtype=I32) * per_block
    block_e = jnp.minimum(jnp.sum((pend[None, :] <= bstart[:, None]).astype(I32), axis=1), N_EXPERTS - 1)
    n_used = pend[-1:] // per_block
    tile_chunk0 = jnp.arange(nt, dtype=I32)[:, None] * STAGE_CHUNKS
    return nch, tile_chunk0 + stage_chunk, run_chunk, block_e.astype(I32), n_used.astype(I32)


def kernel(x, p, norm_mix, w_in, lb_logits, rec_out_gain, sgu_ln_g, sgu_ln_b, sgu_w, sgu_b, w_out, norm_ffn, w_rg, b_rg, w_re, b_re, w_gate, w_up, w_down, norm_pg, w_pg, w_pe, norm_pe, norm_f):
    bsz, seq, d = x.shape
    depth = w_in.shape[0]
    assert bsz == 1 and seq % (MOE_STEP_TILES * TM_MOE) == 0 and seq % TM_MIX == 0
    h = x.reshape(seq, d)

    lb = jnp.cumsum(jax.nn.softmax(lb_logits.astype(F32), axis=0), axis=0)
    lb = lb - lb[0]
    rows = lambda v: v.reshape(depth, 1, -1)
    la, l1, oml = rows(jnp.log(lb)), rows(jnp.log1p(-lb)), rows(1.0 - lb)
    gain = rows(jnp.tile(rec_out_gain, (1, H_REC)))
    sb = jnp.repeat(jnp.swapaxes(sgu_b, 1, 2), D_SGU // H_SGU, axis=2)
    win_b, wout_b, wpg_b, wpe_b = (w.astype(BF16) for w in (w_in, w_out, w_pg, w_pe))
    npad = ROUTER_LANES - N_GROUPS - N_EXPERTS
    wr = jnp.concatenate([w_rg, w_re, jnp.zeros((depth, d, npad), F32)], axis=2)
    wr_hi = wr.astype(BF16)
    wr = jnp.concatenate([wr_hi, (wr - wr_hi.astype(F32)).astype(BF16)], axis=2)
    br = rows(jnp.concatenate([b_rg, b_re, jnp.zeros((depth, npad), F32)], axis=1))
    wsel = jnp.asarray(np.tile(_decay_selectors(), (1, 3)), BF16)
    lev = jnp.asarray(_pair_levels())
    gmix, gffn, gpg, gpe, lng, lnb = (rows(v) for v in (norm_mix, norm_ffn, norm_pg, norm_pe, sgu_ln_g, sgu_ln_b))
    nf = norm_f.reshape(1, d)

    for l in range(depth):
        h = _mixer(l, h, gmix, win_b, la, l1, oml, gain, lng, lnb, sgu_w, sb, wout_b, wsel, lev)
        xs, meta, tab = _router(l, h, gffn, wr, br)
        nch, xs_chunk, run_chunk, block_e, n_used = _routing_tables(tab, seq)
        ctab, ytab = _chunk_tables(block_e, nch, xs_chunk, run_chunk)
        y = _experts(l, block_e, n_used, ctab, xs, w_gate, w_up, w_down)
        h = _combine(l, ytab, h, meta, p, wpe_b, gpe, gpg, wpg_b, nf, y, final=(l == depth - 1))
    return h.reshape(bsz, seq, d)
```
